```python
import jax, jax.numpy as jnp
from jax import lax
import numpy as np

D_MODEL = 1024
BATCH = 32
SEQ = 256
DEPTH = 4
DEC_BATCH = 2
DEC_SEQ = 1024
PAST_LEN = 256

GRID_W = 64
N_EVEN = (DEPTH + 1) // 2
N_ODD = DEPTH // 2
N_MOD = 9
D_FF = 2816
EPS = 1e-6
A_WIDTH = 512
A_GROUPS = 4
A_GROUP_DIM = A_WIDTH // A_GROUPS
A_CHUNK = 128
SSM_HEADS = 8
SSM_HEAD_DIM = 64
SSM_INNER = SSM_HEADS * SSM_HEAD_DIM
SSM_GROUPS = 2
SSM_STATE = 128
SSM_CONV = 3
SSM_CHUNK = 128
SSM_XBC = SSM_INNER + 2 * SSM_GROUPS * SSM_STATE
EVEN_IN = 2 * A_WIDTH + SSM_INNER + SSM_XBC + 2 * SSM_HEADS
EVEN_MIX = A_WIDTH + SSM_INNER
MLA_HEADS = 8
MLA_NOPE = 64
MLA_ROPE = 32
MLA_V = 64
MLA_Q_RANK = 384
MLA_KV_RANK = 256
ROPE_THETA = 10000.0
Q_BLOCK = 128
CONV_WIDTH = 512
CONV_K = 31
ODD_IN = MLA_Q_RANK + MLA_KV_RANK + MLA_ROPE + 2 * CONV_WIDTH
ODD_MIX = MLA_HEADS * MLA_V + CONV_WIDTH

kernel_name = 'hybrid_gmlp_ssd_mla_conformer_dit_step'


def rmsnorm(x, g):
    xf = x.astype(jnp.float32)
    y = xf * lax.rsqrt(jnp.mean(xf * xf, axis=-1, keepdims=True) + EPS)
    return (y * g.astype(jnp.float32)).astype(x.dtype)


def layernorm(x, g, b):
    xf = x.astype(jnp.float32)
    xc = xf - jnp.mean(xf, axis=-1, keepdims=True)
    var = jnp.mean(xc * xc, axis=-1, keepdims=True)
    return (xc * lax.rsqrt(var + EPS) * g.astype(jnp.float32) + b.astype(jnp.float32)).astype(x.dtype)


def modulate(x, shift, scale):
    return x * (1 + scale) + shift


def swiglu(x, w_gu, w_d):
    g, u = jnp.split(x @ w_gu, 2, axis=-1)
    return (jax.nn.silu(g) * u) @ w_d


def ffn_sublayer(x, g, shift, scale, gate, w_gu, w_d):
    return x + 0.5 * gate * swiglu(modulate(rmsnorm(x, g), shift, scale), w_gu, w_d)


def depthwise_conv(x, w, b):
    k = w.shape[0]
    y = lax.conv_general_dilated(x, w[:, None, :].astype(x.dtype), window_strides=(1,),
                                 padding=[(k // 2, k // 2)], dimension_numbers=('NWC', 'WIO', 'NWC'),
                                 feature_group_count=x.shape[-1])
    return y + b


def chunk_gmlp(uv, w_s, b_s, g_v):
    bsz, length, _ = uv.shape
    u, v = jnp.split(jax.nn.gelu(uv), 2, axis=-1)
    v = rmsnorm(v, g_v).reshape(bsz, length // A_CHUNK, A_CHUNK, A_GROUPS, A_GROUP_DIM)
    s = jnp.einsum('gij,bcjgd->bcigd', w_s, v) + b_s.T[None, None, :, :, None]
    return u * s.reshape(bsz, length, A_WIDTH)


def ssd_scan(x, dt, a, bm, cm, h0):
    bsz, length, n_h, p = x.shape
    nc = length // SSM_CHUNK
    rep = n_h // SSM_GROUPS
    bh = jnp.repeat(bm, rep, axis=2).reshape(bsz, nc, SSM_CHUNK, n_h, SSM_STATE)
    ch = jnp.repeat(cm, rep, axis=2).reshape(bsz, nc, SSM_CHUNK, n_h, SSM_STATE)
    xdt = (x * dt[..., None]).reshape(bsz, nc, SSM_CHUNK, n_h, p)
    cum = jnp.cumsum((dt.astype(jnp.float32) * a.astype(jnp.float32)).reshape(bsz, nc, SSM_CHUNK, n_h), axis=2)
    seg = cum[:, :, :, None, :] - cum[:, :, None, :, :]
    lower = jnp.tril(jnp.ones((SSM_CHUNK, SSM_CHUNK), dtype=bool))[None, None, :, :, None]
    lmat = jnp.exp(jnp.where(lower, seg, -jnp.inf)).astype(x.dtype)
    scores = jnp.einsum('bcihn,bcjhn->bcijh', ch, bh) * lmat
    y_diag = jnp.einsum('bcijh,bcjhp->bcihp', scores, xdt)
    decay_end = jnp.exp(cum[:, :, -1:, :] - cum).astype(x.dtype)
    chunk_states = jnp.einsum('bcjhn,bcjh,bcjhp->bchpn', bh, decay_end, xdt)
    chunk_decay = jnp.exp(cum[:, :, -1, :]).astype(x.dtype)

    def step(h, inp):
        dec, st = inp
        return h * dec[:, :, None, None] + st, h

    h_last, h_in = lax.scan(step, h0, (jnp.moveaxis(chunk_decay, 1, 0), jnp.moveaxis(chunk_states, 1, 0)))
    h_in = jnp.moveaxis(h_in, 0, 1)
    y_off = jnp.einsum('bcihn,bchpn->bcihp', ch, h_in) * jnp.exp(cum).astype(x.dtype)[..., None]
    return (y_diag + y_off).reshape(bsz, length, n_h, p), h_last


def ssd_gmlp_mixer(h, w_in, w_out, w_s, b_s, g_v, w_conv, b_conv, dt_bias, a_log, d_skip, g_out, h0_f, h0_b):
    bsz, length, _ = h.shape
    proj = h @ w_in
    o1 = 2 * A_WIDTH
    o2 = o1 + SSM_INNER
    o3 = o2 + SSM_XBC
    y_a = chunk_gmlp(proj[..., :o1], w_s, b_s, g_v)
    z = proj[..., o1:o2]
    xbc = jax.nn.silu(depthwise_conv(proj[..., o2:o3], w_conv, b_conv))
    dt_raw = proj[..., o3:]
    xs = xbc[..., :SSM_INNER].reshape(bsz, length, SSM_HEADS, SSM_HEAD_DIM)
    bm = xbc[..., SSM_INNER:SSM_INNER + SSM_GROUPS * SSM_STATE].reshape(bsz, length, SSM_GROUPS, SSM_STATE)
    cm = xbc[..., SSM_INNER + SSM_GROUPS * SSM_STATE:].reshape(bsz, length, SSM_GROUPS, SSM_STATE)
    dt_f = jax.nn.softplus(dt_raw[..., :SSM_HEADS] + dt_bias[0])
    dt_b = jax.nn.softplus(dt_raw[..., SSM_HEADS:] + dt_bias[1])
    y_f, h_f = ssd_scan(xs, dt_f, -jnp.exp(a_log[0]), bm, cm, h0_f)
    y_b, h_b = ssd_scan(jnp.flip(xs, 1), jnp.flip(dt_b, 1), -jnp.exp(a_log[1]),
                        jnp.flip(bm, 1), jnp.flip(cm, 1), h0_b)
    y = y_f + jnp.flip(y_b, 1) + xs * (d_skip[0] + d_skip[1])[:, None]
    y = rmsnorm(y.reshape(bsz, length, SSM_INNER) * jax.nn.silu(z), g_out)
    return jnp.concatenate([y_a, y], axis=-1) @ w_out, h_f, h_b


def rope_2d(length):
    n_rows = length // GRID_W
    row = jnp.repeat(jnp.arange(n_rows, dtype=jnp.float32), GRID_W)
    col = jnp.tile(jnp.arange(GRID_W, dtype=jnp.float32), n_rows)
    n_freq = MLA_ROPE // 4
    freqs = ROPE_THETA ** (-jnp.arange(n_freq, dtype=jnp.float32) / n_freq)
    ang = jnp.stack([row[:, None] * freqs, col[:, None] * freqs], axis=1)
    return jnp.cos(ang), jnp.sin(ang)


def apply_rope(x, cos, sin):
    xs = x.reshape(x.shape[:-1] + (2, 2, MLA_ROPE // 4))
    x1, x2 = xs[..., 0, :], xs[..., 1, :]
    cos = cos.astype(x.dtype)
    sin = sin.astype(x.dtype)
    out = jnp.stack([x1 * cos - x2 * sin, x1 * sin + x2 * cos], axis=-2)
    return out.reshape(x.shape)


def mla_attend(q_nope, q_rope, k_nope, k_rope, v):
    bsz, lq = q_nope.shape[:2]
    nb = lq // Q_BLOCK
    scale = (MLA_NOPE + MLA_ROPE) ** -0.5
    qn_b = jnp.moveaxis(q_nope.reshape(bsz, nb, Q_BLOCK, MLA_HEADS, MLA_NOPE), 1, 0)
    qr_b = jnp.moveaxis(q_rope.reshape(bsz, nb, Q_BLOCK, MLA_HEADS, MLA_ROPE), 1, 0)

    def block(args):
        qn, qr = args
        s = jnp.einsum('bqhd,bkhd->bhqk', qn, k_nope) + jnp.einsum('bqhd,bkd->bhqk', qr, k_rope)
        p = jax.nn.softmax(s.astype(jnp.float32) * scale, axis=-1).astype(v.dtype)
        return jnp.einsum('bhqk,bkhd->bqhd', p, v)

    o = lax.map(block, (qn_b, qr_b))
    return jnp.moveaxis(o, 0, 1).reshape(bsz, lq, MLA_HEADS * MLA_V)


def up_kv(ckv, w_ukv):
    kv = (ckv @ w_ukv).reshape(ckv.shape[:-1] + (MLA_HEADS, MLA_NOPE + MLA_V))
    return kv[..., :MLA_NOPE], kv[..., MLA_NOPE:]


def mla_conv_mixer(h, w_in, w_out, g_cq, w_uq, g_ckv, w_ukv, w_dw, b_dw, g_ln, b_ln, ctx):
    bsz, length, _ = h.shape
    proj = h @ w_in
    o1 = MLA_Q_RANK
    o2 = o1 + MLA_KV_RANK
    o3 = o2 + MLA_ROPE
    q = (rmsnorm(proj[..., :o1], g_cq) @ w_uq).reshape(bsz, length, MLA_HEADS, MLA_NOPE + MLA_ROPE)
    q_nope, q_rope = q[..., :MLA_NOPE], q[..., MLA_NOPE:]
    ckv = rmsnorm(proj[..., o1:o2], g_ckv)
    k_rope = proj[..., o2:o3]
    k_nope, v = up_kv(ckv, w_ukv)
    if ctx is None:
        attn = mla_attend(q_nope, q_rope, k_nope, k_rope, v)
    else:
        cache_ckv, cache_kr, cos, sin = ctx
        kc_nope, vc = up_kv(cache_ckv, w_ukv)
        attn = mla_attend(q_nope, apply_rope(q_rope, cos[:, None], sin[:, None]),
                          jnp.concatenate([kc_nope, k_nope], axis=1),
                          jnp.concatenate([cache_kr, apply_rope(k_rope, cos, sin)], axis=1),
                          jnp.concatenate([vc, v], axis=1))
    a_half, g_half = jnp.split(proj[..., o3:], 2, axis=-1)
    d = depthwise_conv(a_half * jax.nn.sigmoid(g_half), w_dw, b_dw)
    d = jax.nn.silu(layernorm(d, g_ln, b_ln))
    return jnp.concatenate([attn, d], axis=-1) @ w_out, ckv, k_rope


def setup_inputs(seed: int = 0) -> dict:
    key = jax.random.key(seed)
    ks = jax.random.split(key, 34)
    f32 = jnp.float32

    def nrm(k, shape, scale):
        return jax.random.normal(k, shape, f32) * scale

    dt0 = jnp.exp(jax.random.uniform(ks[19], (N_EVEN, 2, SSM_HEADS), f32,
                                     np.log(1e-3), np.log(1e-1)))
    return {
        'x_prompt': nrm(ks[0], (BATCH, SEQ, D_MODEL), 1.0),
        'x_sample': nrm(ks[1], (DEC_BATCH, DEC_SEQ, D_MODEL), 1.0),
        'state_ssd': nrm(ks[2], (DEC_BATCH, N_EVEN, 2, SSM_HEADS, SSM_HEAD_DIM, SSM_STATE), 0.5),
        'cache_mla_ckv': nrm(ks[3], (DEC_BATCH, N_ODD, PAST_LEN, MLA_KV_RANK), 1.0),
        'cache_mla_krope': nrm(ks[4], (DEC_BATCH, N_ODD, PAST_LEN, MLA_ROPE), 1.0),
        'c': nrm(ks[5], (DEC_BATCH, D_MODEL), 1.0),
        'c_ctx': nrm(ks[6], (D_MODEL,), 1.0),
        'w_mod': nrm(ks[7], (DEPTH, D_MODEL, N_MOD * D_MODEL), 0.5 * D_MODEL ** -0.5),
        'b_mod': nrm(ks[8], (DEPTH, N_MOD * D_MODEL), 0.02),
        'g_norm': 1.0 + nrm(ks[9], (DEPTH, 3, D_MODEL), 0.1),
        'w_ff_gu': nrm(ks[10], (DEPTH, 2, D_MODEL, 2 * D_FF), D_MODEL ** -0.5),
        'w_ff_down': nrm(ks[11], (DEPTH, 2, D_FF, D_MODEL), D_FF ** -0.5),
        'w_in_even': nrm(ks[12], (N_EVEN, D_MODEL, EVEN_IN), D_MODEL ** -0.5),
        'w_out_even': nrm(ks[13], (N_EVEN, EVEN_MIX, D_MODEL), EVEN_MIX ** -0.5),
        'w_spatial': nrm(ks[14], (N_EVEN, A_GROUPS, A_CHUNK, A_CHUNK), A_CHUNK ** -0.5),
        'b_spatial': 1.0 + nrm(ks[15], (N_EVEN, A_GROUPS, A_CHUNK), 0.1),
        'g_gmlp_v': 1.0 + nrm(ks[16], (N_EVEN, A_WIDTH), 0.1),
        'w_conv_ssm': nrm(ks[17], (N_EVEN, SSM_CONV, SSM_XBC), SSM_CONV ** -0.5),
        'b_conv_ssm': nrm(ks[18], (N_EVEN, SSM_XBC), 0.02),
        'dt_bias': dt0 + jnp.log(-jnp.expm1(-dt0)),
        'a_log': jnp.log(jax.random.uniform(ks[20], (N_EVEN, 2, SSM_HEADS), f32, 1.0, 16.0)),
        'd_skip': 1.0 + nrm(ks[21], (N_EVEN, 2, SSM_HEADS), 0.1),
        'g_ssm_out': 1.0 + nrm(ks[22], (N_EVEN, SSM_INNER), 0.1),
        'w_in_odd': nrm(ks[23], (N_ODD, D_MODEL, ODD_IN), D_MODEL ** -0.5),
        'w_out_odd': nrm(ks[24], (N_ODD, ODD_MIX, D_MODEL), ODD_MIX ** -0.5),
        'g_cq': 1.0 + nrm(ks[25], (N_ODD, MLA_Q_RANK), 0.1),
        'w_uq': nrm(ks[26], (N_ODD, MLA_Q_RANK, MLA_HEADS * (MLA_NOPE + MLA_ROPE)), MLA_Q_RANK ** -0.5),
        'g_ckv': 1.0 + nrm(ks[27], (N_ODD, MLA_KV_RANK), 0.1),
        'w_ukv': nrm(ks[28], (N_ODD, MLA_KV_RANK, MLA_HEADS * (MLA_NOPE + MLA_V)), MLA_KV_RANK ** -0.5),
        'w_dwconv': nrm(ks[29], (N_ODD, CONV_K, CONV_WIDTH), CONV_K ** -0.5),
        'b_dwconv': nrm(ks[30], (N_ODD, CONV_WIDTH), 0.02),
        'g_conv_ln': 1.0 + nrm(ks[31], (N_ODD, CONV_WIDTH), 0.1),
        'b_conv_ln': nrm(ks[32], (N_ODD, CONV_WIDTH), 0.02),
        'g_final': 1.0 + nrm(ks[33], (D_MODEL,), 0.1),
    }


def reference(x_prompt, x_sample, state_ssd, cache_mla_ckv, cache_mla_krope, c, c_ctx,
              w_mod, b_mod, g_norm, w_ff_gu, w_ff_down,
              w_in_even, w_out_even, w_spatial, b_spatial, g_gmlp_v, w_conv_ssm, b_conv_ssm,
              dt_bias, a_log, d_skip, g_ssm_out,
              w_in_odd, w_out_odd, g_cq, w_uq, g_ckv, w_ukv, w_dwconv, b_dwconv, g_conv_ln, b_conv_ln,
              g_final):
    bp = x_prompt.shape[0]
    xc, xs = x_prompt, x_sample
    rope_cos, rope_sin = rope_2d(x_sample.shape[1])
    silu_ctx = jax.nn.silu(c_ctx)
    silu_c = jax.nn.silu(c)
    new_ssd, new_ckv, new_kr = [], [], []
    for l in range(DEPTH):
        mc = jnp.split(silu_ctx @ w_mod[l] + b_mod[l], N_MOD, axis=-1)
        ms = jnp.split((silu_c @ w_mod[l] + b_mod[l])[:, None, :], N_MOD, axis=-1)
        xc = ffn_sublayer(xc, g_norm[l, 0], mc[0], mc[1], mc[2], w_ff_gu[l, 0], w_ff_down[l, 0])
        xs = ffn_sublayer(xs, g_norm[l, 0], ms[0], ms[1], ms[2], w_ff_gu[l, 0], w_ff_down[l, 0])
        hc = modulate(rmsnorm(xc, g_norm[l, 1]), mc[3], mc[4])
        hs = modulate(rmsnorm(xs, g_norm[l, 1]), ms[3], ms[4])
        i = l // 2
        if l % 2 == 0:
            ep = (w_in_even[i], w_out_even[i], w_spatial[i], b_spatial[i], g_gmlp_v[i],
                  w_conv_ssm[i], b_conv_ssm[i], dt_bias[i], a_log[i], d_skip[i], g_ssm_out[i])
            h0 = jnp.zeros((bp, SSM_HEADS, SSM_HEAD_DIM, SSM_STATE), hc.dtype)
            oc, hf, hb = ssd_gmlp_mixer(hc, *ep, h0, h0)
            os_, _, _ = ssd_gmlp_mixer(hs, *ep, state_ssd[:, i, 0].astype(hs.dtype),
                                       state_ssd[:, i, 1].astype(hs.dtype))
            new_ssd.append(jnp.stack([hf, hb], axis=1))
        else:
            op = (w_in_odd[i], w_out_odd[i], g_cq[i], w_uq[i], g_ckv[i], w_ukv[i],
                  w_dwconv[i], b_dwconv[i], g_conv_ln[i], b_conv_ln[i])
            oc, ckv, kr = mla_conv_mixer(hc, *op, None)
            os_, _, _ = mla_conv_mixer(hs, *op, (cache_mla_ckv[:, i].astype(hs.dtype),
                                                 cache_mla_krope[:, i].astype(hs.dtype),
                                                 rope_cos, rope_sin))
            new_ckv.append(ckv)
            new_kr.append(kr)
        xc = xc + mc[5] * oc
        xs = xs + ms[5] * os_
        xc = ffn_sublayer(xc, g_norm[l, 2], mc[6], mc[7], mc[8], w_ff_gu[l, 1], w_ff_down[l, 1])
        xs = ffn_sublayer(xs, g_norm[l, 2], ms[6], ms[7], ms[8], w_ff_gu[l, 1], w_ff_down[l, 1])
    y_prompt = rmsnorm(xc, g_final)
    y_sample = rmsnorm(xs, g_final)
    new_state_ssd = jnp.stack(new_ssd, axis=1)
    new_cache_mla_ckv = jnp.stack(new_ckv, axis=1)
    new_cache_mla_krope = jnp.stack(new_kr, axis=1)
    return (y_prompt, y_sample, new_state_ssd, new_cache_mla_ckv, new_cache_mla_krope)
```

```python
import functools

import jax
import jax.numpy as jnp
from jax import lax
from jax.experimental import pallas as pl
from jax.experimental.pallas import tpu as pltpu

F32 = jnp.float32
BF16 = jnp.bfloat16

EPS = 1e-6
N_MOD = 9
LANES = 128
A_GROUPS = 4
A_CHUNK = 128
SSM_HEADS = 8
SSM_HEAD_DIM = 64
SSM_GROUPS = 2
SSM_STATE = 128
SSM_CHUNK = 128
MLA_HEADS = 8
MLA_NOPE = 64
MLA_ROPE = 32
MLA_V = 64
CONV_K = 31
GRID_W = 64
ROPE_THETA = 10000.0

VMEM_LIMIT = 56 * 1024 * 1024


def _dot(a, b):
    return jnp.dot(a, b, preferred_element_type=F32)


def _dot_nt(a, b):
    return lax.dot_general(a, b, (((1,), (1,)), ((), ())), preferred_element_type=F32)


def _dot_tn(a, b):
    return lax.dot_general(a, b, (((0,), (0,)), ((), ())), preferred_element_type=F32)


def _rms(x, g):
    return x * lax.rsqrt(jnp.mean(x * x, axis=-1, keepdims=True) + EPS) * g


def _sigmoid(x):
    return 1.0 / (1.0 + jnp.exp(-x))


def _silu(x):
    return x * _sigmoid(x)


def _mod_index(tile_rows, n_ctx_rows, dec_seq):
    def index(i):
        return jnp.maximum((i * tile_rows - n_ctx_rows) // dec_seq + 1, 0)
    return index


def _resident(shape):
    nd = len(shape)
    return pl.BlockSpec(shape, lambda i: (0,) * nd, pipeline_mode=pl.Buffered(1))


def _mod_kernel(ct_ref, w_ref, b_ref, o_ref, *, n_sets):
    w = w_ref[0]
    s = _silu(ct_ref[...])
    rows = [jnp.sum(w * s[:, r:r + 1], axis=0, keepdims=True) for r in range(n_sets)]
    o_ref[0] = jnp.concatenate(rows, axis=0) + b_ref[0]


def _mod_call(cond_t, w_mod, b_mod):
    depth, d, n = w_mod.shape
    n_sets = cond_t.shape[1]
    tn = n // 4
    return pl.pallas_call(
        functools.partial(_mod_kernel, n_sets=n_sets),
        out_shape=jax.ShapeDtypeStruct((depth, n_sets, n), F32),
        grid=(depth, n // tn),
        in_specs=[
            pl.BlockSpec((d, n_sets), lambda l, j: (0, 0)),
            pl.BlockSpec((1, d, tn), lambda l, j: (l, 0, j)),
            pl.BlockSpec((1, 1, tn), lambda l, j: (l, 0, j)),
        ],
        out_specs=pl.BlockSpec((1, n_sets, tn), lambda l, j: (l, 0, j)),
        compiler_params=pltpu.CompilerParams(
            dimension_semantics=("arbitrary", "arbitrary"), vmem_limit_bytes=VMEM_LIMIT),
        name="adaln_mod",
    )(cond_t, w_mod, b_mod.reshape(depth, 1, n))


def _ffn_kernel(x_ref, mod_ref, g_ref, wgu_ref, wd_ref, *rest, mrow, d_ff, tc, final):
    if final:
        gf_ref, o_ref, y_ref = rest
    else:
        (o_ref,) = rest
    x = x_ref[...]
    m = mod_ref[0]
    shift, scale, gate = m[mrow:mrow + 1], m[mrow + 1:mrow + 2], m[mrow + 2:mrow + 3]
    h = (_rms(x, g_ref[...]) * (1.0 + scale) + shift).astype(BF16)
    acc = jnp.zeros(x.shape, F32)
    for c0 in range(0, d_ff, tc):
        g = _dot(h, wgu_ref[:, c0:c0 + tc])
        u = _dot(h, wgu_ref[:, d_ff + c0:d_ff + c0 + tc])
        a = (_silu(g) * u).astype(BF16)
        acc = acc + _dot(a, wd_ref[c0:c0 + tc, :])
    out = x + 0.5 * gate * acc
    o_ref[...] = out
    if final:
        y_ref[...] = _rms(out, gf_ref[...])


def _ffn_call(x, mod_l, g, w_gu, w_d, g_final, *, mrow, tm, n_ctx_rows, dec_seq):
    t, d = x.shape
    d_ff = w_d.shape[0]
    final = g_final is not None
    in_specs = [
        pl.BlockSpec((tm, d), lambda i: (i, 0)),
        pl.BlockSpec((1, N_MOD, d), lambda i: (_mod_index(tm, n_ctx_rows, dec_seq)(i), 0, 0)),
        _resident((1, d)),
        _resident(w_gu.shape),
        _resident(w_d.shape),
    ]
    args = [x, mod_l, g.reshape(1, d), w_gu, w_d]
    out_shape = jax.ShapeDtypeStruct((t, d), F32)
    out_specs = pl.BlockSpec((tm, d), lambda i: (i, 0))
    if final:
        in_specs.append(_resident((1, d)))
        args.append(g_final.reshape(1, d))
        out_shape = (out_shape, out_shape)
        out_specs = (out_specs, pl.BlockSpec((tm, d), lambda i: (i, 0)))
    return pl.pallas_call(
        functools.partial(_ffn_kernel, mrow=mrow, d_ff=d_ff, tc=256, final=final),
        out_shape=out_shape,
        grid=(t // tm,),
        in_specs=in_specs,
        out_specs=out_specs,
        compiler_params=pltpu.CompilerParams(
            dimension_semantics=("arbitrary",), vmem_limit_bytes=VMEM_LIMIT),
        name="ffn_final" if final else "ffn",
    )(*args)


A_WIDTH = A_GROUPS * LANES
SSM_INNER = SSM_HEADS * SSM_HEAD_DIM
SSM_BC = SSM_GROUPS * SSM_STATE
EV_UV = 0
EV_Z = EV_UV + 2 * A_WIDTH
EV_X = EV_Z + SSM_INNER
EV_B = EV_X + SSM_INNER
EV_C = EV_B + SSM_BC
EV_DT = EV_C + SSM_BC
EV_COLS = EV_DT + LANES


def _gelu_tanh(x):
    return 0.5 * x * (1.0 + jnp.tanh(0.7978845608028654 * (x + 0.044715 * (x * x * x))))


def _softplus(x):
    return jnp.maximum(x, 0.0) + jnp.log1p(jnp.exp(-jnp.abs(x)))


def _split3(v):
    hi = v.astype(BF16)
    r = v - hi.astype(F32)
    mid = r.astype(BF16)
    lo = (r - mid.astype(F32)).astype(BF16)
    return hi, mid, lo


def _expand(parts, e):
    n = parts[0].shape[0]
    y = _dot(jnp.concatenate(parts, axis=0), e)
    out = y[0:n]
    for k in range(1, len(parts)):
        out = out + y[k * n:(k + 1) * n]
    return out


def _even_kernel(x_ref, mod_ref, g_ref, win_ref, wout_ref, ws_ref, bs_ref, gv_ref, wc_ref, bc_ref,
                 dtb_ref, alog_ref, dsk_ref, go_ref, e_ref, *rest, seq_len, n_seq, has_h0, emit_state):
    rest = list(rest)
    h0_ref = rest.pop(0) if has_h0 else None
    o_ref = rest.pop(0)
    st_ref = rest.pop(0) if emit_state else None
    h_ref, mix_ref, xs_ref, b_ref, c_ref, dt_ref, cum_ref, y_ref, state_ref = rest

    rows = seq_len * n_seq
    ck = SSM_CHUNK
    n_chunk = seq_len // ck
    m = mod_ref[0]
    h_ref[...] = (_rms(x_ref[...], g_ref[...]) * (1.0 + m[4:5]) + m[3:4]).astype(BF16)

    vb = _rms(_gelu_tanh(_dot(h_ref[...], win_ref[:, EV_UV + A_WIDTH:EV_Z])), gv_ref[...]).astype(BF16)
    for g in range(A_GROUPS):
        l0 = g * LANES
        u = _gelu_tanh(_dot(h_ref[...], win_ref[:, EV_UV + l0:EV_UV + l0 + LANES]))
        for cc in range(rows // A_CHUNK):
            r0 = cc * A_CHUNK
            s = _dot(ws_ref[g], vb[r0:r0 + A_CHUNK, l0:l0 + LANES]) + bs_ref[g]
            mix_ref[r0:r0 + A_CHUNK, l0:l0 + LANES] = (u[r0:r0 + A_CHUNK] * s).astype(BF16)

    pw = 2 * LANES
    rin = lax.broadcasted_iota(jnp.int32, (rows, pw), 0) % seq_len
    for c0 in range(0, EV_DT - EV_X, pw):
        xbc = _dot(h_ref[...], win_ref[:, EV_X + c0:EV_X + c0 + pw])
        wc = wc_ref[:, c0:c0 + pw]
        prev = jnp.where(rin >= 1, pltpu.roll(xbc, 1, 0), 0.0)
        nxt = jnp.where(rin < seq_len - 1, pltpu.roll(xbc, rows - 1, 0), 0.0)
        xbc = _silu(prev * wc[0:1] + xbc * wc[1:2] + nxt * wc[2:3] + bc_ref[:, c0:c0 + pw])
        if c0 < SSM_INNER:
            xs_ref[:, c0:c0 + pw] = xbc
        elif c0 < SSM_INNER + SSM_BC:
            b_ref[:, c0 - SSM_INNER:c0 - SSM_INNER + pw] = xbc.astype(BF16)
        else:
            c_ref[:, c0 - SSM_INNER - SSM_BC:c0 - SSM_INNER - SSM_BC + pw] = xbc.astype(BF16)

    dt = _softplus(_dot(h_ref[...], win_ref[:, EV_DT:EV_COLS]) + dtb_ref[...])
    lane = lax.broadcasted_iota(jnp.int32, dt.shape, 1)
    rck = lax.broadcasted_iota(jnp.int32, dt.shape, 0) % ck
    da = jnp.where(lane < 2 * SSM_HEADS, dt * -jnp.exp(alog_ref[...]), 0.0)
    cf = da
    cb = da
    sh = 1
    while sh < ck:
        cf = cf + jnp.where(rck >= sh, pltpu.roll(cf, sh, 0), 0.0)
        cb = cb + jnp.where(rck < ck - sh, pltpu.roll(cb, rows - sh, 0), 0.0)
        sh *= 2
    dt_ref[...] = dt
    cum_ref[...] = jnp.where(lane < SSM_HEADS, cf, cb)

    ii = lax.broadcasted_iota(jnp.int32, (ck, ck), 0)
    jj = lax.broadcasted_iota(jnp.int32, (ck, ck), 1)
    lane_lo = lax.broadcasted_iota(jnp.int32, (ck, LANES), 1) < SSM_HEAD_DIM
    gw = SSM_INNER // SSM_GROUPS
    hpg = SSM_HEADS // SSM_GROUPS

    def chunk_step(r0, direction, first_write):
        e = e_ref[:, direction * SSM_INNER:(direction + 1) * SSM_INNER]
        cum_n = cum_ref[pl.ds(r0, ck), :]
        cum_t = cum_n.T
        dt_n = dt_ref[pl.ds(r0, ck), :]
        dt_e = _expand(_split3(dt_n), e)
        cum_e = _expand(_split3(cum_n), e)
        x_c = xs_ref[pl.ds(r0, ck), :]
        xdt = x_c * dt_e
        if direction == 0:
            tot = cum_e[ck - 1:ck, :]
            tri = ii >= jj
        else:
            tot = cum_e[0:1, :]
            tri = ii <= jj
        ec = jnp.exp(cum_e)
        xd = (xdt * jnp.exp(tot - cum_e)).astype(BF16)
        bm = b_ref[pl.ds(r0, ck), :]
        cm = c_ref[pl.ds(r0, ck), :]
        st = state_ref[...]
        y_parts = []
        st_parts = []
        for g in range(SSM_GROUPS):
            bg = bm[:, g * SSM_STATE:(g + 1) * SSM_STATE]
            cg = cm[:, g * SSM_STATE:(g + 1) * SSM_STATE]
            cbm = _dot_nt(cg, bg)
            for pr in range(hpg // 2):
                hd = g * hpg + 2 * pr
                l0 = hd * SSM_HEAD_DIM
                sc = []
                for k in range(2):
                    col = direction * SSM_HEADS + hd + k
                    seg = cum_n[:, col:col + 1] - cum_t[col:col + 1, :]
                    sc.append((cbm * jnp.where(tri, jnp.exp(seg), 0.0)).astype(BF16))
                xp = xdt[:, l0:l0 + LANES]
                rhs = jnp.concatenate([jnp.where(lane_lo, xp, 0.0), jnp.where(lane_lo, 0.0, xp)],
                                      axis=0).astype(BF16)
                y_parts.append(_dot(jnp.concatenate(sc, axis=1), rhs))
            y_off = _dot(cg, st[:, g * gw:(g + 1) * gw].astype(BF16))
            y_parts[-2] = y_parts[-2] + y_off[:, 0:LANES] * ec[:, g * gw:g * gw + LANES]
            y_parts[-1] = y_parts[-1] + y_off[:, LANES:gw] * ec[:, g * gw + LANES:(g + 1) * gw]
            st_parts.append(_dot_tn(bg, xd[:, g * gw:(g + 1) * gw]))
        y = jnp.concatenate(y_parts, axis=1)
        if first_write:
            y_ref[pl.ds(r0, ck), :] = y
        else:
            y_ref[pl.ds(r0, ck), :] += y
        state_ref[...] = st * jnp.exp(tot) + jnp.concatenate(st_parts, axis=1)

    for s in range(n_seq):
        for direction in range(2):
            if has_h0:
                state_ref[...] = h0_ref[s, 0, direction].T
            else:
                state_ref[...] = jnp.zeros(state_ref.shape, F32)

            def body(k, carry, s=s, direction=direction):
                c = k if direction == 0 else n_chunk - 1 - k
                chunk_step(pl.multiple_of(s * seq_len + c * ck, ck), direction, direction == 0)
                return carry

            lax.fori_loop(0, n_chunk, body, 0)
            if emit_state:
                st_ref[s, 0, direction] = state_ref[...].T

    z = _dot(h_ref[...], win_ref[:, EV_Z:EV_X])
    y = y_ref[...] + xs_ref[...] * (dsk_ref[0:1] + dsk_ref[1:2])
    mix_ref[:, A_WIDTH:] = _rms(y * _silu(z), go_ref[...]).astype(BF16)
    for c0 in range(0, o_ref.shape[1], pw):
        o_ref[:, c0:c0 + pw] = (x_ref[:, c0:c0 + pw]
                                + m[5:6, c0:c0 + pw] * _dot(mix_ref[...], wout_ref[:, c0:c0 + pw]))


def _even_call(x, mod_l, g, p, h0, *, layer_idx, seq_len, n_seq, row0, n_tiles, mod_set, batch):
    t, d = x.shape
    rows = seq_len * n_seq
    tile0 = row0 // rows
    has_h0 = h0 is not None
    emit_state = not has_h0
    hp, ns = SSM_INNER, SSM_STATE
    in_specs = [
        pl.BlockSpec((rows, d), lambda i: (tile0 + i, 0)),
        pl.BlockSpec((1, N_MOD, d), lambda i: (mod_set(i), 0, 0)),
        _resident((1, d)),
        _resident(p["w_in"].shape),
        _resident(p["w_out"].shape),
        _resident(p["w_s"].shape),
        _resident(p["b_s"].shape),
        _resident(p["g_v"].shape),
        _resident(p["w_conv"].shape),
        _resident(p["b_conv"].shape),
        _resident(p["dt_bias"].shape),
        _resident(p["a_log"].shape),
        _resident(p["d_skip"].shape),
        _resident(p["g_out"].shape),
        _resident(p["expand"].shape),
    ]
    args = [x, mod_l, g.reshape(1, d), p["w_in"], p["w_out"], p["w_s"], p["b_s"], p["g_v"], p["w_conv"],
            p["b_conv"], p["dt_bias"], p["a_log"], p["d_skip"], p["g_out"], p["expand"]]
    out_shape = [jax.ShapeDtypeStruct((t, d), F32)]
    out_specs = [pl.BlockSpec((rows, d), lambda i: (tile0 + i, 0))]
    if has_h0:
        in_specs.append(pl.BlockSpec((n_seq, 1, 2, hp, ns), lambda i: (i, layer_idx, 0, 0, 0)))
        args.append(h0)
    if emit_state:
        out_shape.append(jax.ShapeDtypeStruct((batch, 1, 2, hp, ns), F32))
        out_specs.append(pl.BlockSpec((n_seq, 1, 2, hp, ns), lambda i: (i, 0, 0, 0, 0)))
    scratch = [
        pltpu.VMEM((rows, d), BF16),
        pltpu.VMEM((rows, A_WIDTH + SSM_INNER), BF16),
        pltpu.VMEM((rows, SSM_INNER), F32),
        pltpu.VMEM((rows, SSM_BC), BF16),
        pltpu.VMEM((rows, SSM_BC), BF16),
        pltpu.VMEM((rows, LANES), F32),
        pltpu.VMEM((rows, LANES), F32),
        pltpu.VMEM((rows, SSM_INNER), F32),
        pltpu.VMEM((SSM_STATE, SSM_INNER), F32),
    ]
    res = pl.pallas_call(
        functools.partial(_even_kernel, seq_len=seq_len, n_seq=n_seq, has_h0=has_h0, emit_state=emit_state),
        out_shape=out_shape,
        grid=(n_tiles,),
        in_specs=in_specs,
        out_specs=out_specs,
        scratch_shapes=scratch,
        input_output_aliases={0: 0},
        compiler_params=pltpu.CompilerParams(
            dimension_semantics=("arbitrary",), vmem_limit_bytes=VMEM_LIMIT),
        name="even_mixer_latent" if has_h0 else "even_mixer_context",
    )(*args)
    return res


def _pack_even(w_in, w_out, w_s, b_s, g_v, w_conv, b_conv, dt_bias, a_log, d_skip, g_out):
    d = w_in.shape[0]
    n_dt = 2 * SSM_HEADS
    w_in_p = jnp.concatenate([w_in[:, :EV_DT], w_in[:, EV_DT:], jnp.zeros((d, LANES - n_dt), F32)], axis=1)

    def narrow(v):
        return jnp.concatenate([v.reshape(1, n_dt), jnp.zeros((1, LANES - n_dt), F32)], axis=1)

    lane_head = jnp.arange(2 * SSM_INNER) // SSM_HEAD_DIM
    expand = (jnp.arange(LANES)[:, None] == lane_head[None, :]).astype(BF16)
    return {
        "w_in": w_in_p.astype(BF16),
        "w_out": w_out.astype(BF16),
        "w_s": w_s.astype(BF16),
        "b_s": jnp.broadcast_to(b_s[:, :, None], b_s.shape + (LANES,)),
        "g_v": g_v.reshape(1, -1),
        "w_conv": w_conv,
        "b_conv": b_conv.reshape(1, -1),
        "dt_bias": narrow(dt_bias),
        "a_log": narrow(a_log),
        "d_skip": jnp.repeat(d_skip, SSM_HEAD_DIM, axis=1),
        "g_out": g_out.reshape(1, -1),
        "expand": expand,
    }


MLA_Q_RANK = 384
MLA_KV_RANK = 256
CONV_WIDTH = 512
HEAD_W = LANES
ROPE_L0 = MLA_NOPE
OD_CQ = 0
OD_CKV = OD_CQ + MLA_Q_RANK
OD_A = OD_CKV + MLA_KV_RANK
OD_KR = OD_A + 2 * CONV_WIDTH
OD_KRS = OD_KR + HEAD_W
OD_COLS = OD_KRS + HEAD_W
CONV_PAD = 16
CONV_RB = 64
ATT_QB = 256


def _odd_kernel(x_ref, mod_ref, g_ref, win_ref, wuq_ref, wukv_ref, wout_ref, gcq_ref, gckv_ref,
                wdw_ref, bdw_ref, gln_ref, bln_ref, *rest, seq_len, n_seq, past):
    rest = list(rest)
    has_cache = past > 0
    if has_cache:
        cckv_ref, ckr_ref, cos_ref, sin_ref = rest[:4]
        o_ref = rest[4]
        rest = rest[5:]
    else:
        o_ref, ckv_out_ref, kr_out_ref = rest[:3]
        rest = rest[3:]
    h_ref, cq_ref, ckv_ref, kr_ref, mix_ref, q_ref, k_ref, v_ref, oh_ref, pad_ref = rest

    nh = MLA_HEADS
    hw = nh * HEAD_W
    n_keys = past + seq_len
    scale = (MLA_NOPE + MLA_ROPE) ** -0.5
    m = mod_ref[0]
    h_ref[...] = (_rms(x_ref[...], g_ref[...]) * (1.0 + m[4:5]) + m[3:4]).astype(BF16)

    cq_ref[...] = _rms(_dot(h_ref[...], win_ref[:, OD_CQ:OD_CKV]), gcq_ref[...]).astype(BF16)
    ckv = _rms(_dot(h_ref[...], win_ref[:, OD_CKV:OD_A]), gckv_ref[...])
    ckv_ref[...] = ckv.astype(BF16)
    kr = _dot(h_ref[...], win_ref[:, OD_KR:OD_KRS])
    if has_cache:
        cos = jnp.concatenate([cos_ref[...]] * n_seq, axis=0)
        sin = jnp.concatenate([sin_ref[...]] * n_seq, axis=0)
        kr = kr * cos + _dot(h_ref[...], win_ref[:, OD_KRS:OD_COLS]) * sin
    else:
        for s in range(n_seq):
            ckv_out_ref[s, 0] = ckv[s * seq_len:(s + 1) * seq_len]
            kr_out_ref[s, 0] = kr[s * seq_len:(s + 1) * seq_len, ROPE_L0:ROPE_L0 + MLA_ROPE]
    kr_ref[...] = kr

    def q_body(hd, carry):
        q = _dot(cq_ref[...], wuq_ref[hd])
        if has_cache:
            q = q * cos + _dot(cq_ref[...], wuq_ref[nh + hd]) * sin
        q_ref[hd] = q.astype(BF16)
        return carry

    lax.fori_loop(0, nh, q_body, 0)

    n_qb = seq_len // ATT_QB
    for s in range(n_seq):
        r0 = s * seq_len

        def kv_body(hd, carry, s=s, r0=r0):
            lat = ckv_ref[r0:r0 + seq_len]
            k_ref[hd, past:n_keys] = (_dot(lat, wukv_ref[hd]) + kr_ref[r0:r0 + seq_len]).astype(BF16)
            v_ref[hd, past:n_keys] = _dot(lat, wukv_ref[nh + hd]).astype(BF16)
            if has_cache:
                lat_c = cckv_ref[s, 0].astype(BF16)
                k_ref[hd, 0:past] = (_dot(lat_c, wukv_ref[hd]) + ckr_ref[s, 0]).astype(BF16)
                v_ref[hd, 0:past] = _dot(lat_c, wukv_ref[nh + hd]).astype(BF16)
            return carry

        lax.fori_loop(0, nh, kv_body, 0)

        def att_body(it, carry, r0=r0):
            hd = it // n_qb
            q0 = pl.multiple_of(r0 + (it % n_qb) * ATT_QB, ATT_QB)
            sc = _dot_nt(q_ref[hd, pl.ds(q0, ATT_QB), :], k_ref[hd]) * scale
            p = jnp.exp(sc - jnp.max(sc, axis=-1, keepdims=True))
            den = jnp.sum(p, axis=-1, keepdims=True)
            oh_ref[hd, pl.ds(q0, ATT_QB), :] = (_dot(p.astype(BF16), v_ref[hd]) / den).astype(BF16)
            return carry

        lax.fori_loop(0, nh * n_qb, att_body, 0)

    for hd in range(nh):
        mix_ref[:, hd * HEAD_W:(hd + 1) * HEAD_W] = oh_ref[hd]

    pw = 2 * LANES
    zpad = jnp.zeros((CONV_PAD, CONV_WIDTH), F32)
    span = CONV_RB + 2 * CONV_PAD
    for s in range(n_seq):
        r0 = s * seq_len
        pad_ref[0:CONV_PAD] = zpad
        pad_ref[CONV_PAD + seq_len:2 * CONV_PAD + seq_len] = zpad
        for c0 in range(0, CONV_WIDTH, pw):
            a = _dot(h_ref[r0:r0 + seq_len], win_ref[:, OD_A + c0:OD_A + c0 + pw])
            gt = _dot(h_ref[r0:r0 + seq_len], win_ref[:, OD_A + CONV_WIDTH + c0:OD_A + CONV_WIDTH + c0 + pw])
            pad_ref[CONV_PAD:CONV_PAD + seq_len, c0:c0 + pw] = a * _sigmoid(gt)

        def conv_body(b, carry, r0=r0):
            base = pl.multiple_of(b * CONV_RB, CONV_RB)
            blk = pad_ref[pl.ds(base, span), :]
            acc = jnp.broadcast_to(bdw_ref[...], (CONV_RB, CONV_WIDTH))
            for r in range(8):
                sh = blk[r:r + span - 8]
                for qq in range((span - CONV_RB) // 8):
                    k = 8 * qq + r - 1
                    if 0 <= k < CONV_K:
                        acc = acc + wdw_ref[k:k + 1, :] * sh[8 * qq:8 * qq + CONV_RB]
            dc = acc - jnp.mean(acc, axis=-1, keepdims=True)
            var = jnp.mean(dc * dc, axis=-1, keepdims=True)
            dn = dc * lax.rsqrt(var + EPS) * gln_ref[...] + bln_ref[...]
            mix_ref[pl.ds(r0 + base, CONV_RB), hw:] = _silu(dn).astype(BF16)
            return carry

        lax.fori_loop(0, seq_len // CONV_RB, conv_body, 0)

    for c0 in range(0, o_ref.shape[1], pw):
        o_ref[:, c0:c0 + pw] = (x_ref[:, c0:c0 + pw]
                                + m[5:6, c0:c0 + pw] * _dot(mix_ref[...], wout_ref[:, c0:c0 + pw]))


def _odd_call(x, mod_l, g, p, cache, *, layer_idx, seq_len, n_seq, row0, n_tiles, mod_set, batch):
    t, d = x.shape
    rows = seq_len * n_seq
    tile0 = row0 // rows
    has_cache = cache is not None
    past = cache[0].shape[2] if has_cache else 0
    names = ["w_in", "w_uq", "w_ukv", "w_out", "g_cq", "g_ckv", "w_dw", "b_dw", "g_ln", "b_ln"]
    in_specs = [
        pl.BlockSpec((rows, d), lambda i: (tile0 + i, 0)),
        pl.BlockSpec((1, N_MOD, d), lambda i: (mod_set(i), 0, 0)),
        _resident((1, d)),
    ] + [_resident(p[k].shape) for k in names]
    args = [x, mod_l, g.reshape(1, d)] + [p[k] for k in names]
    out_shape = [jax.ShapeDtypeStruct((t, d), F32)]
    out_specs = [pl.BlockSpec((rows, d), lambda i: (tile0 + i, 0))]
    if has_cache:
        cckv, ckr, cos, sin = cache
        in_specs += [
            pl.BlockSpec((n_seq, 1, past, MLA_KV_RANK), lambda i: (i, layer_idx, 0, 0)),
            pl.BlockSpec((n_seq, 1, past, HEAD_W), lambda i: (i, layer_idx, 0, 0)),
            _resident(cos.shape),
            _resident(sin.shape),
        ]
        args += [cckv, ckr, cos, sin]
    else:
        out_shape += [jax.ShapeDtypeStruct((batch, 1, seq_len, MLA_KV_RANK), F32),
                      jax.ShapeDtypeStruct((batch, 1, seq_len, MLA_ROPE), F32)]
        out_specs += [pl.BlockSpec((n_seq, 1, seq_len, MLA_KV_RANK), lambda i: (i, 0, 0, 0)),
                      pl.BlockSpec((n_seq, 1, seq_len, MLA_ROPE), lambda i: (i, 0, 0, 0))]
    n_keys = past + seq_len
    hw = MLA_HEADS * HEAD_W
    scratch = [
        pltpu.VMEM((rows, d), BF16),
        pltpu.VMEM((rows, MLA_Q_RANK), BF16),
        pltpu.VMEM((rows, MLA_KV_RANK), BF16),
        pltpu.VMEM((rows, HEAD_W), F32),
        pltpu.VMEM((rows, hw + CONV_WIDTH), BF16),
        pltpu.VMEM((MLA_HEADS, rows, HEAD_W), BF16),
        pltpu.VMEM((MLA_HEADS, n_keys, HEAD_W), BF16),
        pltpu.VMEM((MLA_HEADS, n_keys, HEAD_W), BF16),
        pltpu.VMEM((MLA_HEADS, rows, HEAD_W), BF16),
        pltpu.VMEM((seq_len + 2 * CONV_PAD, CONV_WIDTH), F32),
    ]
    return pl.pallas_call(
        functools.partial(_odd_kernel, seq_len=seq_len, n_seq=n_seq, past=past),
        out_shape=out_shape,
        grid=(n_tiles,),
        in_specs=in_specs,
        out_specs=out_specs,
        scratch_shapes=scratch,
        input_output_aliases={0: 0},
        compiler_params=pltpu.CompilerParams(
            dimension_semantics=("arbitrary",), vmem_limit_bytes=VMEM_LIMIT),
        name="odd_mixer_latent" if has_cache else "odd_mixer_context",
    )(*args)


def _head_tiles(w, per_head, start, width, lane0):
    k = w.shape[0]
    blk = w.reshape(k, MLA_HEADS, per_head)[:, :, start:start + width]
    out = jnp.zeros((k, MLA_HEADS, HEAD_W), w.dtype).at[:, :, lane0:lane0 + width].set(blk)
    return out.transpose(1, 0, 2)


def _rope_swap(w):
    q = MLA_ROPE // 4
    idx = jnp.concatenate([jnp.arange(q, 2 * q), jnp.arange(0, q), jnp.arange(3 * q, 4 * q), jnp.arange(2 * q, 3 * q)])
    return w[..., idx]


def _pack_odd(w_in, w_out, g_cq, w_uq, g_ckv, w_ukv, w_dw, b_dw, g_ln, b_ln):
    d = w_in.shape[0]
    o1 = MLA_Q_RANK
    o2 = o1 + MLA_KV_RANK
    o3 = o2 + MLA_ROPE
    w_kr = w_in[:, o2:o3]

    def kr_tile(w):
        return jnp.zeros((d, HEAD_W), F32).at[:, ROPE_L0:ROPE_L0 + MLA_ROPE].set(w)

    w_in_p = jnp.concatenate([w_in[:, :o2], w_in[:, o3:], kr_tile(w_kr), kr_tile(_rope_swap(w_kr))], axis=1)
    qh = MLA_NOPE + MLA_ROPE
    rank = w_uq.shape[0]
    w_q3 = w_uq.reshape(rank, MLA_HEADS, qh)
    w_q_sw = jnp.concatenate([jnp.zeros((rank, MLA_HEADS, MLA_NOPE), F32), _rope_swap(w_q3[:, :, MLA_NOPE:])],
                             axis=2).reshape(rank, MLA_HEADS * qh)
    w_uq_p = jnp.concatenate([_head_tiles(w_uq, qh, 0, qh, 0),
                              _head_tiles(w_q_sw, qh, MLA_NOPE, MLA_ROPE, ROPE_L0)], axis=0)
    kvh = MLA_NOPE + MLA_V
    w_ukv_p = jnp.concatenate([_head_tiles(w_ukv, kvh, 0, MLA_NOPE, 0),
                               _head_tiles(w_ukv, kvh, MLA_NOPE, MLA_V, 0)], axis=0)
    n_att = MLA_HEADS * MLA_V
    w_att = jnp.zeros((MLA_HEADS, HEAD_W, d), F32).at[:, :MLA_V, :].set(w_out[:n_att].reshape(MLA_HEADS, MLA_V, d))
    w_out_p = jnp.concatenate([w_att.reshape(MLA_HEADS * HEAD_W, d), w_out[n_att:]], axis=0)
    return {
        "w_in": w_in_p.astype(BF16),
        "w_uq": w_uq_p.astype(BF16),
        "w_ukv": w_ukv_p.astype(BF16),
        "w_out": w_out_p.astype(BF16),
        "g_cq": g_cq.reshape(1, -1),
        "g_ckv": g_ckv.reshape(1, -1),
        "w_dw": w_dw,
        "b_dw": b_dw.reshape(1, -1),
        "g_ln": g_ln.reshape(1, -1),
        "b_ln": b_ln.reshape(1, -1),
    }


def _rope_tables(length):
    n_freq = MLA_ROPE // 4
    pos = jnp.arange(length)
    row = (pos // GRID_W).astype(F32)
    col = (pos % GRID_W).astype(F32)
    freqs = ROPE_THETA ** (-jnp.arange(n_freq, dtype=F32) / n_freq)
    ar = row[:, None] * freqs
    ac = col[:, None] * freqs
    cos = jnp.concatenate([jnp.cos(ar), jnp.cos(ar), jnp.cos(ac), jnp.cos(ac)], axis=1)
    sin = jnp.concatenate([-jnp.sin(ar), jnp.sin(ar), -jnp.sin(ac), jnp.sin(ac)], axis=1)
    tail = HEAD_W - ROPE_L0 - MLA_ROPE
    cos_t = jnp.concatenate([jnp.ones((length, ROPE_L0), F32), cos, jnp.zeros((length, tail), F32)], axis=1)
    sin_t = jnp.concatenate([jnp.zeros((length, ROPE_L0), F32), sin, jnp.zeros((length, tail), F32)], axis=1)
    return cos_t, sin_t


def kernel(x_prompt, x_sample, state_ssd, cache_mla_ckv, cache_mla_krope, c, c_ctx, w_mod, b_mod, g_norm, w_ff_gu, w_ff_down, w_in_even, w_out_even, w_spatial, b_spatial, g_gmlp_v, w_conv_ssm, b_conv_ssm, dt_bias, a_log, d_skip, g_ssm_out, w_in_odd, w_out_odd, g_cq, w_uq, g_ckv, w_ukv, w_dwconv, b_dwconv, g_conv_ln, b_conv_ln, g_final):
    bp, seq, d = x_prompt.shape
    db, dec_seq, _ = x_sample.shape
    n_ctx = bp * seq
    x = jnp.concatenate([x_prompt.reshape(n_ctx, d), x_sample.reshape(db * dec_seq, d)], axis=0)
    cond_t = jnp.concatenate([c_ctx[None], c], axis=0).T
    mod = _mod_call(cond_t, w_mod, b_mod).reshape(w_mod.shape[0], 1 + db, N_MOD, d)
    depth = w_mod.shape[0]
    ffn = functools.partial(_ffn_call, tm=512, n_ctx_rows=n_ctx, dec_seq=dec_seq)
    ctx_seqs = 2
    ctx_tile = dict(seq_len=seq, n_seq=ctx_seqs, row0=0, n_tiles=bp // ctx_seqs, mod_set=lambda i: 0, batch=bp)
    lat_tile = dict(seq_len=dec_seq, n_seq=1, row0=n_ctx, n_tiles=db, mod_set=lambda i: 1 + i, batch=bp)
    h0 = state_ssd.reshape(db, state_ssd.shape[1], 2, SSM_INNER, SSM_STATE)
    ckr_tile = jnp.pad(cache_mla_krope, ((0, 0), (0, 0), (0, 0), (ROPE_L0, HEAD_W - ROPE_L0 - MLA_ROPE)))
    cos_t, sin_t = _rope_tables(dec_seq)
    new_ssd, new_ckv, new_kr = [], [], []
    y = None
    for l in range(depth):
        i = l // 2
        last = l == depth - 1
        x = ffn(x, mod[l], g_norm[l, 0], w_ff_gu[l, 0].astype(BF16), w_ff_down[l, 0].astype(BF16), None, mrow=0)
        if l % 2 == 0:
            pe = _pack_even(w_in_even[i], w_out_even[i], w_spatial[i], b_spatial[i], g_gmlp_v[i], w_conv_ssm[i],
                            b_conv_ssm[i], dt_bias[i], a_log[i], d_skip[i], g_ssm_out[i])
            x, st = _even_call(x, mod[l], g_norm[l, 1], pe, None, layer_idx=i, **ctx_tile)
            (x,) = _even_call(x, mod[l], g_norm[l, 1], pe, h0, layer_idx=i, **lat_tile)
            new_ssd.append(st)
        else:
            po = _pack_odd(w_in_odd[i], w_out_odd[i], g_cq[i], w_uq[i], g_ckv[i], w_ukv[i], w_dwconv[i],
                           b_dwconv[i], g_conv_ln[i], b_conv_ln[i])
            x, ckv, kr = _odd_call(x, mod[l], g_norm[l, 1], po, None, layer_idx=i, **ctx_tile)
            (x,) = _odd_call(x, mod[l], g_norm[l, 1], po, (cache_mla_ckv, ckr_tile, cos_t, sin_t),
                             layer_idx=i, **lat_tile)
            new_ckv.append(ckv)
            new_kr.append(kr)
        res = ffn(x, mod[l], g_norm[l, 2], w_ff_gu[l, 1].astype(BF16), w_ff_down[l, 1].astype(BF16),
                  g_final if last else None, mrow=6)
        if last:
            x, y = res
        else:
            x = res
    new_state = jnp.concatenate(new_ssd, axis=1).reshape(
        bp, len(new_ssd), 2, SSM_HEADS, SSM_HEAD_DIM, SSM_STATE)
    return (y[:n_ctx].reshape(bp, seq, d), y[n_ctx:].reshape(db, dec_seq, d), new_state,
            jnp.concatenate(new_ckv, axis=1), jnp.concatenate(new_kr, axis=1))
```

```python
import functools

import jax
import jax.numpy as jnp
from jax import lax
from jax.experimental import pallas as pl
from jax.experimental.pallas import tpu as pltpu

F32 = jnp.float32
BF16 = jnp.bfloat16

EPS = 1e-6
N_MOD = 9
LANES = 128
A_GROUPS = 4
A_CHUNK = 128
SSM_HEADS = 8
SSM_HEAD_DIM = 64
SSM_GROUPS = 2
SSM_STATE = 128
SSM_CHUNK = 128
MLA_HEADS = 8
MLA_NOPE = 64
MLA_ROPE = 32
MLA_V = 64
CONV_K = 31
GRID_W = 64
ROPE_THETA = 10000.0

VMEM_LIMIT = 56 * 1024 * 1024


def _dot(a, b):
    return jnp.dot(a, b, preferred_element_type=F32)


def _dot_nt(a, b):
    return lax.dot_general(a, b, (((1,), (1,)), ((), ())), preferred_element_type=F32)


def _dot_tn(a, b):
    return lax.dot_general(a, b, (((0,), (0,)), ((), ())), preferred_element_type=F32)


def _rms(x, g):
    return x * lax.rsqrt(jnp.mean(x * x, axis=-1, keepdims=True) + EPS) * g


def _sigmoid(x):
    return 1.0 / (1.0 + jnp.exp(-x))


def _silu(x):
    return x * _sigmoid(x)


def _mod_index(tile_rows, n_ctx_rows, dec_seq):
    def index(i):
        return jnp.maximum((i * tile_rows - n_ctx_rows) // dec_seq + 1, 0)
    return index


def _resident(shape):
    nd = len(shape)
    return pl.BlockSpec(shape, lambda i: (0,) * nd, pipeline_mode=pl.Buffered(1))


def _mod_kernel(ct_ref, w_ref, b_ref, o_ref, *, n_sets):
    w = w_ref[0]
    s = _silu(ct_ref[...])
    rows = [jnp.sum(w * s[:, r:r + 1], axis=0, keepdims=True) for r in range(n_sets)]
    o_ref[0] = jnp.concatenate(rows, axis=0) + b_ref[0]


def _mod_call(cond_t, w_mod, b_mod):
    depth, d, n = w_mod.shape
    n_sets = cond_t.shape[1]
    tn = n // 4
    return pl.pallas_call(
        functools.partial(_mod_kernel, n_sets=n_sets),
        out_shape=jax.ShapeDtypeStruct((depth, n_sets, n), F32),
        grid=(depth, n // tn),
        in_specs=[
            pl.BlockSpec((d, n_sets), lambda l, j: (0, 0)),
            pl.BlockSpec((1, d, tn), lambda l, j: (l, 0, j)),
            pl.BlockSpec((1, 1, tn), lambda l, j: (l, 0, j)),
        ],
        out_specs=pl.BlockSpec((1, n_sets, tn), lambda l, j: (l, 0, j)),
        compiler_params=pltpu.CompilerParams(
            dimension_semantics=("arbitrary", "arbitrary"), vmem_limit_bytes=VMEM_LIMIT),
        name="adaln_mod",
    )(cond_t, w_mod, b_mod.reshape(depth, 1, n))


FFN_CHUNK = 2 * LANES


def _ffn_kernel(*refs, mrow, d_ff, ctx_tiles, first, final):
    refs = list(refs)
    n_x = 2 if first else 1
    x_refs, (mod_ref, g_ref, wgu_ref, wd_ref), refs = refs[:n_x], refs[n_x:n_x + 4], refs[n_x + 4:]
    i = pl.program_id(0)
    if first:
        x = jnp.where(i < ctx_tiles, x_refs[0][...], x_refs[1][...])
    else:
        x = x_refs[0][...]
    m = mod_ref[0]
    shift, scale, gate = m[mrow:mrow + 1], m[mrow + 1:mrow + 2], m[mrow + 2:mrow + 3]
    h = (_rms(x, g_ref[...]) * (1.0 + scale) + shift).astype(BF16)
    acc = jnp.zeros(x.shape, F32)
    for c0 in range(0, d_ff, FFN_CHUNK):
        g = _dot(h, wgu_ref[0, 0, :, c0:c0 + FFN_CHUNK].astype(BF16))
        u = _dot(h, wgu_ref[0, 0, :, d_ff + c0:d_ff + c0 + FFN_CHUNK].astype(BF16))
        a = (_silu(g) * u).astype(BF16)
        acc = acc + _dot(a, wd_ref[0, 0, c0:c0 + FFN_CHUNK, :].astype(BF16))
    out = x + 0.5 * gate * acc
    if final:
        gf_ref, yc_ref, yl_ref = refs
        y = _rms(out, gf_ref[...])

        @pl.when(i < ctx_tiles)
        def _():
            yc_ref[...] = y

        @pl.when(i >= ctx_tiles)
        def _():
            yl_ref[...] = y
    else:
        (o_ref,) = refs
        o_ref[...] = out


def _ffn_call(xs, mod_l, g, w_gu, w_d, g_final, *, layer, which, mrow, tm, n_ctx_rows, dec_seq):
    first = len(xs) == 2
    final = g_final is not None
    d = xs[0].shape[1]
    t = sum(a.shape[0] for a in xs)
    d_ff = w_d.shape[2]
    ctx_tiles = n_ctx_rows // tm

    def ctx_map(i):
        return (jnp.minimum(i, ctx_tiles - 1), 0)

    def lat_map(i):
        return (jnp.maximum(i - ctx_tiles, 0), 0)

    x_specs = ([pl.BlockSpec((tm, d), ctx_map), pl.BlockSpec((tm, d), lat_map)] if first
               else [pl.BlockSpec((tm, d), lambda i: (i, 0))])
    in_specs = x_specs + [
        pl.BlockSpec((1, N_MOD, d), lambda i: (_mod_index(tm, n_ctx_rows, dec_seq)(i), 0, 0)),
        _resident((1, d)),
        pl.BlockSpec((1, 1) + w_gu.shape[2:], lambda i: (layer, which, 0, 0), pipeline_mode=pl.Buffered(1)),
        pl.BlockSpec((1, 1) + w_d.shape[2:], lambda i: (layer, which, 0, 0), pipeline_mode=pl.Buffered(1)),
    ]
    args = list(xs) + [mod_l, g.reshape(1, d), w_gu, w_d]
    if final:
        in_specs.append(_resident((1, d)))
        args.append(g_final.reshape(1, d))
        out_shape = (jax.ShapeDtypeStruct((n_ctx_rows, d), F32), jax.ShapeDtypeStruct((t - n_ctx_rows, d), F32))
        out_specs = (pl.BlockSpec((tm, d), ctx_map), pl.BlockSpec((tm, d), lat_map))
    else:
        out_shape = jax.ShapeDtypeStruct((t, d), F32)
        out_specs = pl.BlockSpec((tm, d), lambda i: (i, 0))
    return pl.pallas_call(
        functools.partial(_ffn_kernel, mrow=mrow, d_ff=d_ff, ctx_tiles=ctx_tiles, first=first, final=final),
        out_shape=out_shape,
        grid=(t // tm,),
        in_specs=in_specs,
        out_specs=out_specs,
        compiler_params=pltpu.CompilerParams(
            dimension_semantics=("arbitrary",), vmem_limit_bytes=VMEM_LIMIT),
        name="ffn_final" if final else ("ffn_first" if first else "ffn"),
    )(*args)


A_WIDTH = A_GROUPS * LANES
SSM_INNER = SSM_HEADS * SSM_HEAD_DIM
SSM_BC = SSM_GROUPS * SSM_STATE
EV_UV = 0
EV_Z = EV_UV + 2 * A_WIDTH
EV_X = EV_Z + SSM_INNER
EV_B = EV_X + SSM_INNER
EV_C = EV_B + SSM_BC
EV_DT = EV_C + SSM_BC
EV_COLS = EV_DT + LANES


def _gelu_tanh(x):
    return 0.5 * x * (1.0 + jnp.tanh(0.7978845608028654 * (x + 0.044715 * (x * x * x))))


def _softplus(x):
    return jnp.maximum(x, 0.0) + jnp.log1p(jnp.exp(-jnp.abs(x)))


def _split3(v):
    hi = v.astype(BF16)
    r = v - hi.astype(F32)
    mid = r.astype(BF16)
    lo = (r - mid.astype(F32)).astype(BF16)
    return hi, mid, lo


def _expand(parts, e):
    n = parts[0].shape[0]
    y = _dot(jnp.concatenate(parts, axis=0), e)
    out = y[0:n]
    for k in range(1, len(parts)):
        out = out + y[k * n:(k + 1) * n]
    return out


def _even_kernel(x_ref, mod_ref, g_ref, win_ref, wout_ref, ws_ref, bs_ref, gv_ref, wc_ref, bc_ref,
                 dtb_ref, alog_ref, dsk_ref, go_ref, e_ref, *rest, seq_len, n_seq, has_h0, emit_state):
    rest = list(rest)
    h0_ref = rest.pop(0) if has_h0 else None
    o_ref = rest.pop(0)
    st_ref = rest.pop(0) if emit_state else None
    h_ref, mix_ref, xs_ref, b_ref, c_ref, dt_ref, cum_ref, y_ref, state_ref = rest

    rows = seq_len * n_seq
    ck = SSM_CHUNK
    n_chunk = seq_len // ck
    m = mod_ref[0]
    h_ref[...] = (_rms(x_ref[...], g_ref[...]) * (1.0 + m[4:5]) + m[3:4]).astype(BF16)

    vb = _rms(_gelu_tanh(_dot(h_ref[...], win_ref[:, EV_UV + A_WIDTH:EV_Z])), gv_ref[...]).astype(BF16)
    for g in range(A_GROUPS):
        l0 = g * LANES
        u = _gelu_tanh(_dot(h_ref[...], win_ref[:, EV_UV + l0:EV_UV + l0 + LANES]))
        for cc in range(rows // A_CHUNK):
            r0 = cc * A_CHUNK
            s = _dot(ws_ref[g], vb[r0:r0 + A_CHUNK, l0:l0 + LANES]) + bs_ref[g]
            mix_ref[r0:r0 + A_CHUNK, l0:l0 + LANES] = (u[r0:r0 + A_CHUNK] * s).astype(BF16)

    pw = 2 * LANES
    rin = lax.broadcasted_iota(jnp.int32, (rows, pw), 0) % seq_len
    for c0 in range(0, EV_DT - EV_X, pw):
        xbc = _dot(h_ref[...], win_ref[:, EV_X + c0:EV_X + c0 + pw])
        wc = wc_ref[:, c0:c0 + pw]
        prev = jnp.where(rin >= 1, pltpu.roll(xbc, 1, 0), 0.0)
        nxt = jnp.where(rin < seq_len - 1, pltpu.roll(xbc, rows - 1, 0), 0.0)
        xbc = _silu(prev * wc[0:1] + xbc * wc[1:2] + nxt * wc[2:3] + bc_ref[:, c0:c0 + pw])
        if c0 < SSM_INNER:
            xs_ref[:, c0:c0 + pw] = xbc
        elif c0 < SSM_INNER + SSM_BC:
            b_ref[:, c0 - SSM_INNER:c0 - SSM_INNER + pw] = xbc.astype(BF16)
        else:
            c_ref[:, c0 - SSM_INNER - SSM_BC:c0 - SSM_INNER - SSM_BC + pw] = xbc.astype(BF16)

    dt = _softplus(_dot(h_ref[...], win_ref[:, EV_DT:EV_COLS]) + dtb_ref[...])
    lane = lax.broadcasted_iota(jnp.int32, dt.shape, 1)
    rck = lax.broadcasted_iota(jnp.int32, dt.shape, 0) % ck
    da = jnp.where(lane < 2 * SSM_HEADS, dt * -jnp.exp(alog_ref[...]), 0.0)
    cf = da
    cb = da
    sh = 1
    while sh < ck:
        cf = cf + jnp.where(rck >= sh, pltpu.roll(cf, sh, 0), 0.0)
        cb = cb + jnp.where(rck < ck - sh, pltpu.roll(cb, rows - sh, 0), 0.0)
        sh *= 2
    dt_ref[...] = dt
    cum_ref[...] = jnp.where(lane < SSM_HEADS, cf, cb)

    ii = lax.broadcasted_iota(jnp.int32, (ck, ck), 0)
    jj = lax.broadcasted_iota(jnp.int32, (ck, ck), 1)
    lane_lo = lax.broadcasted_iota(jnp.int32, (ck, LANES), 1) < SSM_HEAD_DIM
    gw = SSM_INNER // SSM_GROUPS
    hpg = SSM_HEADS // SSM_GROUPS

    def chunk_step(r0, direction, slot):
        e = e_ref[:, direction * SSM_INNER:(direction + 1) * SSM_INNER]
        cum_n = cum_ref[pl.ds(r0, ck), :]
        cum_t = cum_n.T
        dt_n = dt_ref[pl.ds(r0, ck), :]
        dt_e = jnp.concatenate(
            [jnp.where(lane_lo,
                       jnp.broadcast_to(dt_n[:, direction * SSM_HEADS + hd:direction * SSM_HEADS + hd + 1], (ck, LANES)),
                       jnp.broadcast_to(dt_n[:, direction * SSM_HEADS + hd + 1:direction * SSM_HEADS + hd + 2], (ck, LANES)))
             for hd in range(0, SSM_HEADS, 2)], axis=1)
        cum_e = _expand(_split3(cum_n), e)
        x_c = xs_ref[pl.ds(r0, ck), :]
        xdt = x_c * dt_e
        if direction == 0:
            tot = cum_e[ck - 1:ck, :]
            tri = ii >= jj
        else:
            tot = cum_e[0:1, :]
            tri = ii <= jj
        ec = jnp.exp(cum_e)
        xd = (xdt * jnp.exp(tot - cum_e)).astype(BF16)
        bm = b_ref[pl.ds(r0, ck), :]
        cm = c_ref[pl.ds(r0, ck), :]
        st = state_ref[slot]
        y_parts = []
        st_parts = []
        for g in range(SSM_GROUPS):
            bg = bm[:, g * SSM_STATE:(g + 1) * SSM_STATE]
            cg = cm[:, g * SSM_STATE:(g + 1) * SSM_STATE]
            cbm = _dot_nt(cg, bg)
            for pr in range(hpg // 2):
                hd = g * hpg + 2 * pr
                l0 = hd * SSM_HEAD_DIM
                sc = []
                for k in range(2):
                    col = direction * SSM_HEADS + hd + k
                    seg = cum_n[:, col:col + 1] - cum_t[col:col + 1, :]
                    sc.append((cbm * jnp.where(tri, jnp.exp(seg), 0.0)).astype(BF16))
                xp = xdt[:, l0:l0 + LANES]
                rhs = jnp.concatenate([jnp.where(lane_lo, xp, 0.0), jnp.where(lane_lo, 0.0, xp)],
                                      axis=0).astype(BF16)
                y_parts.append(_dot(jnp.concatenate(sc, axis=1), rhs))
            y_off = _dot(cg, st[:, g * gw:(g + 1) * gw].astype(BF16))
            y_parts[-2] = y_parts[-2] + y_off[:, 0:LANES] * ec[:, g * gw:g * gw + LANES]
            y_parts[-1] = y_parts[-1] + y_off[:, LANES:gw] * ec[:, g * gw + LANES:(g + 1) * gw]
            st_parts.append(_dot_tn(bg, xd[:, g * gw:(g + 1) * gw]))
        y_ref[direction, pl.ds(r0, ck), :] = jnp.concatenate(y_parts, axis=1)
        state_ref[slot] = st * jnp.exp(tot) + jnp.concatenate(st_parts, axis=1)

    for s in range(n_seq):
        for direction in range(2):
            if has_h0:
                state_ref[2 * s + direction] = h0_ref[s, 0, direction].T
            else:
                state_ref[2 * s + direction] = jnp.zeros(state_ref.shape[1:], F32)

    def scan_body(k, carry):
        for s in range(n_seq):
            for direction in range(2):
                c = k if direction == 0 else n_chunk - 1 - k
                chunk_step(pl.multiple_of(s * seq_len + c * ck, ck), direction, 2 * s + direction)
        return carry

    lax.fori_loop(0, n_chunk, scan_body, 0)
    if emit_state:
        for s in range(n_seq):
            for direction in range(2):
                st_ref[s, 0, direction] = state_ref[2 * s + direction].T

    z = _dot(h_ref[...], win_ref[:, EV_Z:EV_X])
    y = y_ref[0] + y_ref[1] + xs_ref[...] * (dsk_ref[0:1] + dsk_ref[1:2])
    mix_ref[:, A_WIDTH:] = _rms(y * _silu(z), go_ref[...]).astype(BF16)
    for c0 in range(0, o_ref.shape[1], pw):
        o_ref[:, c0:c0 + pw] = (x_ref[:, c0:c0 + pw]
                                + m[5:6, c0:c0 + pw] * _dot(mix_ref[...], wout_ref[:, c0:c0 + pw]))


def _even_call(x, mod_l, g, p, h0, *, layer_idx, seq_len, n_seq, row0, n_tiles, mod_set, batch):
    t, d = x.shape
    rows = seq_len * n_seq
    tile0 = row0 // rows
    has_h0 = h0 is not None
    emit_state = not has_h0
    hp, ns = SSM_INNER, SSM_STATE
    in_specs = [
        pl.BlockSpec((rows, d), lambda i: (tile0 + i, 0)),
        pl.BlockSpec((1, N_MOD, d), lambda i: (mod_set(i), 0, 0)),
        _resident((1, d)),
        _resident(p["w_in"].shape),
        _resident(p["w_out"].shape),
        _resident(p["w_s"].shape),
        _resident(p["b_s"].shape),
        _resident(p["g_v"].shape),
        _resident(p["w_conv"].shape),
        _resident(p["b_conv"].shape),
        _resident(p["dt_bias"].shape),
        _resident(p["a_log"].shape),
        _resident(p["d_skip"].shape),
        _resident(p["g_out"].shape),
        _resident(p["expand"].shape),
    ]
    args = [x, mod_l, g.reshape(1, d), p["w_in"], p["w_out"], p["w_s"], p["b_s"], p["g_v"], p["w_conv"],
            p["b_conv"], p["dt_bias"], p["a_log"], p["d_skip"], p["g_out"], p["expand"]]
    out_shape = [jax.ShapeDtypeStruct((t, d), F32)]
    out_specs = [pl.BlockSpec((rows, d), lambda i: (tile0 + i, 0))]
    if has_h0:
        in_specs.append(pl.BlockSpec((n_seq, 1, 2, hp, ns), lambda i: (i, layer_idx, 0, 0, 0)))
        args.append(h0)
    if emit_state:
        out_shape.append(jax.ShapeDtypeStruct((batch, 1, 2, hp, ns), F32))
        out_specs.append(pl.BlockSpec((n_seq, 1, 2, hp, ns), lambda i: (i, 0, 0, 0, 0)))
    scratch = [
        pltpu.VMEM((rows, d), BF16),
        pltpu.VMEM((rows, A_WIDTH + SSM_INNER), BF16),
        pltpu.VMEM((rows, SSM_INNER), F32),
        pltpu.VMEM((rows, SSM_BC), BF16),
        pltpu.VMEM((rows, SSM_BC), BF16),
        pltpu.VMEM((rows, LANES), F32),
        pltpu.VMEM((rows, LANES), F32),
        pltpu.VMEM((2, rows, SSM_INNER), F32),
        pltpu.VMEM((2 * n_seq, SSM_STATE, SSM_INNER), F32),
    ]
    res = pl.pallas_call(
        functools.partial(_even_kernel, seq_len=seq_len, n_seq=n_seq, has_h0=has_h0, emit_state=emit_state),
        out_shape=out_shape,
        grid=(n_tiles,),
        in_specs=in_specs,
        out_specs=out_specs,
        scratch_shapes=scratch,
        input_output_aliases={0: 0},
        compiler_params=pltpu.CompilerParams(
            dimension_semantics=("arbitrary",), vmem_limit_bytes=VMEM_LIMIT),
        name="even_mixer_latent" if has_h0 else "even_mixer_context",
    )(*args)
    return res


def _pack_even(w_in, w_out, w_s, b_s, g_v, w_conv, b_conv, dt_bias, a_log, d_skip, g_out):
    d = w_in.shape[0]
    n_dt = 2 * SSM_HEADS
    w_in_p = jnp.concatenate([w_in[:, :EV_DT], w_in[:, EV_DT:], jnp.zeros((d, LANES - n_dt), F32)], axis=1)

    def narrow(v):
        return jnp.concatenate([v.reshape(1, n_dt), jnp.zeros((1, LANES - n_dt), F32)], axis=1)

    lane_head = jnp.arange(2 * SSM_INNER) // SSM_HEAD_DIM
    expand = (jnp.arange(LANES)[:, None] == lane_head[None, :]).astype(BF16)
    return {
        "w_in": w_in_p.astype(BF16),
        "w_out": w_out.astype(BF16),
        "w_s": w_s.astype(BF16),
        "b_s": jnp.broadcast_to(b_s[:, :, None], b_s.shape + (LANES,)),
        "g_v": g_v.reshape(1, -1),
        "w_conv": w_conv,
        "b_conv": b_conv.reshape(1, -1),
        "dt_bias": narrow(dt_bias),
        "a_log": narrow(a_log),
        "d_skip": jnp.repeat(d_skip, SSM_HEAD_DIM, axis=1),
        "g_out": g_out.reshape(1, -1),
        "expand": expand,
    }


MLA_Q_RANK = 384
MLA_KV_RANK = 256
CONV_WIDTH = 512
HEAD_W = LANES
ROPE_L0 = MLA_NOPE
OD_CQ = 0
OD_CKV = OD_CQ + MLA_Q_RANK
OD_A = OD_CKV + MLA_KV_RANK
OD_KR = OD_A + 2 * CONV_WIDTH
OD_KRS = OD_KR + HEAD_W
OD_COLS = OD_KRS + HEAD_W
CONV_PAD = 16
CONV_RB = 64
ATT_QB = 256


def _odd_kernel(x_ref, mod_ref, g_ref, win_ref, wuq_ref, wukv_ref, wout_ref, gcq_ref, gckv_ref,
                wdw_ref, bdw_ref, gln_ref, bln_ref, *rest, seq_len, n_seq, past):
    rest = list(rest)
    has_cache = past > 0
    if has_cache:
        cckv_ref, ckr_ref, cos_ref, sin_ref = rest[:4]
        o_ref = rest[4]
        rest = rest[5:]
    else:
        o_ref, ckv_out_ref, kr_out_ref = rest[:3]
        rest = rest[3:]
    h_ref, cq_ref, ckv_ref, kr_ref, mix_ref, q_ref, k_ref, v_ref, oh_ref, pad_ref = rest

    nh = MLA_HEADS
    hw = nh * HEAD_W
    n_keys = past + seq_len
    scale = (MLA_NOPE + MLA_ROPE) ** -0.5
    m = mod_ref[0]
    h_ref[...] = (_rms(x_ref[...], g_ref[...]) * (1.0 + m[4:5]) + m[3:4]).astype(BF16)

    cq_ref[...] = _rms(_dot(h_ref[...], win_ref[:, OD_CQ:OD_CKV]), gcq_ref[...]).astype(BF16)
    ckv = _rms(_dot(h_ref[...], win_ref[:, OD_CKV:OD_A]), gckv_ref[...])
    ckv_ref[...] = ckv.astype(BF16)
    kr = _dot(h_ref[...], win_ref[:, OD_KR:OD_KRS])
    if has_cache:
        cos = jnp.concatenate([cos_ref[...]] * n_seq, axis=0)
        sin = jnp.concatenate([sin_ref[...]] * n_seq, axis=0)
        kr = kr * cos + _dot(h_ref[...], win_ref[:, OD_KRS:OD_COLS]) * sin
    else:
        for s in range(n_seq):
            ckv_out_ref[s, 0] = ckv[s * seq_len:(s + 1) * seq_len]
            kr_out_ref[s, 0] = kr[s * seq_len:(s + 1) * seq_len, ROPE_L0:ROPE_L0 + MLA_ROPE]
    kr_ref[...] = kr

    def q_body(hd, carry):
        q = _dot(cq_ref[...], wuq_ref[hd])
        if has_cache:
            q = q * cos + _dot(cq_ref[...], wuq_ref[nh + hd]) * sin
        q_ref[hd] = q.astype(BF16)
        return carry

    lax.fori_loop(0, nh, q_body, 0, unroll=4)

    n_qb = seq_len // ATT_QB
    for s in range(n_seq):
        r0 = s * seq_len

        def kv_body(hd, carry, s=s, r0=r0):
            lat = ckv_ref[r0:r0 + seq_len]
            k_ref[hd, past:n_keys] = (_dot(lat, wukv_ref[hd]) + kr_ref[r0:r0 + seq_len]).astype(BF16)
            v_ref[hd, past:n_keys] = _dot(lat, wukv_ref[nh + hd]).astype(BF16)
            if has_cache:
                lat_c = cckv_ref[s, 0].astype(BF16)
                k_ref[hd, 0:past] = (_dot(lat_c, wukv_ref[hd]) + ckr_ref[s, 0]).astype(BF16)
                v_ref[hd, 0:past] = _dot(lat_c, wukv_ref[nh + hd]).astype(BF16)
            return carry

        lax.fori_loop(0, nh, kv_body, 0, unroll=4)

        def att_body(it, carry, r0=r0):
            hd = it // n_qb
            q0 = pl.multiple_of(r0 + (it % n_qb) * ATT_QB, ATT_QB)
            sc = _dot_nt(q_ref[hd, pl.ds(q0, ATT_QB), :], k_ref[hd]) * scale
            p = jnp.exp(sc - jnp.max(sc, axis=-1, keepdims=True))
            den = jnp.sum(p, axis=-1, keepdims=True)
            oh_ref[hd, pl.ds(q0, ATT_QB), :] = (_dot(p.astype(BF16), v_ref[hd]) / den).astype(BF16)
            return carry

        lax.fori_loop(0, nh * n_qb, att_body, 0, unroll=8)

    for hd in range(nh):
        mix_ref[:, hd * HEAD_W:(hd + 1) * HEAD_W] = oh_ref[hd]

    pw = 2 * LANES
    zpad = jnp.zeros((CONV_PAD, CONV_WIDTH), F32)
    span = CONV_RB + 2 * CONV_PAD
    for s in range(n_seq):
        r0 = s * seq_len
        pad_ref[0:CONV_PAD] = zpad
        pad_ref[CONV_PAD + seq_len:2 * CONV_PAD + seq_len] = zpad
        for c0 in range(0, CONV_WIDTH, pw):
            a = _dot(h_ref[r0:r0 + seq_len], win_ref[:, OD_A + c0:OD_A + c0 + pw])
            gt = _dot(h_ref[r0:r0 + seq_len], win_ref[:, OD_A + CONV_WIDTH + c0:OD_A + CONV_WIDTH + c0 + pw])
            pad_ref[CONV_PAD:CONV_PAD + seq_len, c0:c0 + pw] = a * _sigmoid(gt)

        def conv_body(b, carry, r0=r0):
            base = pl.multiple_of(b * CONV_RB, CONV_RB)
            parts = []
            for l0 in range(0, CONV_WIDTH, LANES):
                blk = pad_ref[pl.ds(base, span), l0:l0 + LANES]
                acc = jnp.broadcast_to(bdw_ref[:, l0:l0 + LANES], (CONV_RB, LANES))
                for r in range(8):
                    sh = blk if r == 0 else pltpu.roll(blk, span - r, 0)
                    for qq in range((span - CONV_RB) // 8):
                        k = 8 * qq + r - 1
                        if 0 <= k < CONV_K:
                            acc = acc + wdw_ref[k:k + 1, l0:l0 + LANES] * sh[8 * qq:8 * qq + CONV_RB]
                parts.append(acc)
            acc = jnp.concatenate(parts, axis=1)
            dc = acc - jnp.mean(acc, axis=-1, keepdims=True)
            var = jnp.mean(dc * dc, axis=-1, keepdims=True)
            dn = dc * lax.rsqrt(var + EPS) * gln_ref[...] + bln_ref[...]
            mix_ref[pl.ds(r0 + base, CONV_RB), hw:] = _silu(dn).astype(BF16)
            return carry

        lax.fori_loop(0, seq_len // CONV_RB, conv_body, 0)

    for c0 in range(0, o_ref.shape[1], pw):
        o_ref[:, c0:c0 + pw] = (x_ref[:, c0:c0 + pw]
                                + m[5:6, c0:c0 + pw] * _dot(mix_ref[...], wout_ref[:, c0:c0 + pw]))


def _odd_call(x, mod_l, g, p, cache, *, layer_idx, seq_len, n_seq, row0, n_tiles, mod_set, batch):
    t, d = x.shape
    rows = seq_len * n_seq
    tile0 = row0 // rows
    has_cache = cache is not None
    past = cache[0].shape[2] if has_cache else 0
    names = ["w_in", "w_uq", "w_ukv", "w_out", "g_cq", "g_ckv", "w_dw", "b_dw", "g_ln", "b_ln"]
    in_specs = [
        pl.BlockSpec((rows, d), lambda i: (tile0 + i, 0)),
        pl.BlockSpec((1, N_MOD, d), lambda i: (mod_set(i), 0, 0)),
        _resident((1, d)),
    ] + [_resident(p[k].shape) for k in names]
    args = [x, mod_l, g.reshape(1, d)] + [p[k] for k in names]
    out_shape = [jax.ShapeDtypeStruct((t, d), F32)]
    out_specs = [pl.BlockSpec((rows, d), lambda i: (tile0 + i, 0))]
    if has_cache:
        cckv, ckr, cos, sin = cache
        in_specs += [
            pl.BlockSpec((n_seq, 1, past, MLA_KV_RANK), lambda i: (i, layer_idx, 0, 0)),
            pl.BlockSpec((n_seq, 1, past, HEAD_W), lambda i: (i, layer_idx, 0, 0)),
            _resident(cos.shape),
            _resident(sin.shape),
        ]
        args += [cckv, ckr, cos, sin]
    else:
        out_shape += [jax.ShapeDtypeStruct((batch, 1, seq_len, MLA_KV_RANK), F32),
                      jax.ShapeDtypeStruct((batch, 1, seq_len, MLA_ROPE), F32)]
        out_specs += [pl.BlockSpec((n_seq, 1, seq_len, MLA_KV_RANK), lambda i: (i, 0, 0, 0)),
                      pl.BlockSpec((n_seq, 1, seq_len, MLA_ROPE), lambda i: (i, 0, 0, 0))]
    n_keys = past + seq_len
    hw = MLA_HEADS * HEAD_W
    scratch = [
        pltpu.VMEM((rows, d), BF16),
        pltpu.VMEM((rows, MLA_Q_RANK), BF16),
        pltpu.VMEM((rows, MLA_KV_RANK), BF16),
        pltpu.VMEM((rows, HEAD_W), F32),
        pltpu.VMEM((rows, hw + CONV_WIDTH), BF16),
        pltpu.VMEM((MLA_HEADS, rows, HEAD_W), BF16),
        pltpu.VMEM((MLA_HEADS, n_keys, HEAD_W), BF16),
        pltpu.VMEM((MLA_HEADS, n_keys, HEAD_W), BF16),
        pltpu.VMEM((MLA_HEADS, rows, HEAD_W), BF16),
        pltpu.VMEM((seq_len + 2 * CONV_PAD, CONV_WIDTH), F32),
    ]
    return pl.pallas_call(
        functools.partial(_odd_kernel, seq_len=seq_len, n_seq=n_seq, past=past),
        out_shape=out_shape,
        grid=(n_tiles,),
        in_specs=in_specs,
        out_specs=out_specs,
        scratch_shapes=scratch,
        input_output_aliases={0: 0},
        compiler_params=pltpu.CompilerParams(
            dimension_semantics=("arbitrary",), vmem_limit_bytes=VMEM_LIMIT),
        name="odd_mixer_latent" if has_cache else "odd_mixer_context",
    )(*args)


def _head_tiles(w, per_head, start, width, lane0):
    k = w.shape[0]
    blk = w.reshape(k, MLA_HEADS, per_head)[:, :, start:start + width]
    out = jnp.zeros((k, MLA_HEADS, HEAD_W), w.dtype).at[:, :, lane0:lane0 + width].set(blk)
    return out.transpose(1, 0, 2)


def _rope_swap(w):
    q = MLA_ROPE // 4
    idx = jnp.concatenate([jnp.arange(q, 2 * q), jnp.arange(0, q), jnp.arange(3 * q, 4 * q), jnp.arange(2 * q, 3 * q)])
    return w[..., idx]


def _pack_odd(w_in, w_out, g_cq, w_uq, g_ckv, w_ukv, w_dw, b_dw, g_ln, b_ln):
    d = w_in.shape[0]
    o1 = MLA_Q_RANK
    o2 = o1 + MLA_KV_RANK
    o3 = o2 + MLA_ROPE
    w_kr = w_in[:, o2:o3]

    def kr_tile(w):
        return jnp.zeros((d, HEAD_W), F32).at[:, ROPE_L0:ROPE_L0 + MLA_ROPE].set(w)

    w_in_p = jnp.concatenate([w_in[:, :o2], w_in[:, o3:], kr_tile(w_kr), kr_tile(_rope_swap(w_kr))], axis=1)
    qh = MLA_NOPE + MLA_ROPE
    rank = w_uq.shape[0]
    w_q3 = w_uq.reshape(rank, MLA_HEADS, qh)
    w_q_sw = jnp.concatenate([jnp.zeros((rank, MLA_HEADS, MLA_NOPE), F32), _rope_swap(w_q3[:, :, MLA_NOPE:])],
                             axis=2).reshape(rank, MLA_HEADS * qh)
    w_uq_p = jnp.concatenate([_head_tiles(w_uq, qh, 0, qh, 0),
                              _head_tiles(w_q_sw, qh, MLA_NOPE, MLA_ROPE, ROPE_L0)], axis=0)
    kvh = MLA_NOPE + MLA_V
    w_ukv_p = jnp.concatenate([_head_tiles(w_ukv, kvh, 0, MLA_NOPE, 0),
                               _head_tiles(w_ukv, kvh, MLA_NOPE, MLA_V, 0)], axis=0)
    n_att = MLA_HEADS * MLA_V
    w_att = jnp.zeros((MLA_HEADS, HEAD_W, d), F32).at[:, :MLA_V, :].set(w_out[:n_att].reshape(MLA_HEADS, MLA_V, d))
    w_out_p = jnp.concatenate([w_att.reshape(MLA_HEADS * HEAD_W, d), w_out[n_att:]], axis=0)
    return {
        "w_in": w_in_p.astype(BF16),
        "w_uq": w_uq_p.astype(BF16),
        "w_ukv": w_ukv_p.astype(BF16),
        "w_out": w_out_p.astype(BF16),
        "g_cq": g_cq.reshape(1, -1),
        "g_ckv": g_ckv.reshape(1, -1),
        "w_dw": w_dw,
        "b_dw": b_dw.reshape(1, -1),
        "g_ln": g_ln.reshape(1, -1),
        "b_ln": b_ln.reshape(1, -1),
    }


def _rope_tables(length):
    n_freq = MLA_ROPE // 4
    pos = jnp.arange(length)
    row = (pos // GRID_W).astype(F32)
    col = (pos % GRID_W).astype(F32)
    freqs = ROPE_THETA ** (-jnp.arange(n_freq, dtype=F32) / n_freq)
    ar = row[:, None] * freqs
    ac = col[:, None] * freqs
    cos = jnp.concatenate([jnp.cos(ar), jnp.cos(ar), jnp.cos(ac), jnp.cos(ac)], axis=1)
    sin = jnp.concatenate([-jnp.sin(ar), jnp.sin(ar), -jnp.sin(ac), jnp.sin(ac)], axis=1)
    tail = HEAD_W - ROPE_L0 - MLA_ROPE
    cos_t = jnp.concatenate([jnp.ones((length, ROPE_L0), F32), cos, jnp.zeros((length, tail), F32)], axis=1)
    sin_t = jnp.concatenate([jnp.zeros((length, ROPE_L0), F32), sin, jnp.zeros((length, tail), F32)], axis=1)
    return cos_t, sin_t


def kernel(x_prompt, x_sample, state_ssd, cache_mla_ckv, cache_mla_krope, c, c_ctx, w_mod, b_mod, g_norm, w_ff_gu, w_ff_down, w_in_even, w_out_even, w_spatial, b_spatial, g_gmlp_v, w_conv_ssm, b_conv_ssm, dt_bias, a_log, d_skip, g_ssm_out, w_in_odd, w_out_odd, g_cq, w_uq, g_ckv, w_ukv, w_dwconv, b_dwconv, g_conv_ln, b_conv_ln, g_final):
    bp, seq, d = x_prompt.shape
    db, dec_seq, _ = x_sample.shape
    n_ctx = bp * seq
    cond_t = jnp.concatenate([c_ctx[None], c], axis=0).T
    mod = _mod_call(cond_t, w_mod, b_mod).reshape(w_mod.shape[0], 1 + db, N_MOD, d)
    depth = w_mod.shape[0]
    ffn = functools.partial(_ffn_call, tm=512, n_ctx_rows=n_ctx, dec_seq=dec_seq)
    ctx_seqs = 2
    ctx_tile = dict(seq_len=seq, n_seq=ctx_seqs, row0=0, n_tiles=bp // ctx_seqs, mod_set=lambda i: 0, batch=bp)
    lat_tile = dict(seq_len=dec_seq, n_seq=1, row0=n_ctx, n_tiles=db, mod_set=lambda i: 1 + i, batch=bp)
    h0 = state_ssd.reshape(db, state_ssd.shape[1], 2, SSM_INNER, SSM_STATE)
    ckr_tile = jnp.pad(cache_mla_krope, ((0, 0), (0, 0), (0, 0), (ROPE_L0, HEAD_W - ROPE_L0 - MLA_ROPE)))
    cos_t, sin_t = _rope_tables(dec_seq)
    new_ssd, new_ckv, new_kr = [], [], []
    xs = [x_prompt.reshape(n_ctx, d), x_sample.reshape(db * dec_seq, d)]
    for l in range(depth):
        i = l // 2
        last = l == depth - 1
        x = ffn(xs, mod[l], g_norm[l, 0], w_ff_gu, w_ff_down, None, layer=l, which=0, mrow=0)
        if l % 2 == 0:
            pe = _pack_even(w_in_even[i], w_out_even[i], w_spatial[i], b_spatial[i], g_gmlp_v[i], w_conv_ssm[i],
                            b_conv_ssm[i], dt_bias[i], a_log[i], d_skip[i], g_ssm_out[i])
            x, st = _even_call(x, mod[l], g_norm[l, 1], pe, None, layer_idx=i, **ctx_tile)
            (x,) = _even_call(x, mod[l], g_norm[l, 1], pe, h0, layer_idx=i, **lat_tile)
            new_ssd.append(st)
        else:
            po = _pack_odd(w_in_odd[i], w_out_odd[i], g_cq[i], w_uq[i], g_ckv[i], w_ukv[i], w_dwconv[i],
                           b_dwconv[i], g_conv_ln[i], b_conv_ln[i])
            x, ckv, kr = _odd_call(x, mod[l], g_norm[l, 1], po, None, layer_idx=i, **ctx_tile)
            (x,) = _odd_call(x, mod[l], g_norm[l, 1], po, (cache_mla_ckv, ckr_tile, cos_t, sin_t),
                             layer_idx=i, **lat_tile)
            new_ckv.append(ckv)
            new_kr.append(kr)
        res = ffn([x], mod[l], g_norm[l, 2], w_ff_gu, w_ff_down, g_final if last else None,
                  layer=l, which=1, mrow=6)
        xs = [res]
    y_ctx, y_lat = res
    new_state = jnp.concatenate(new_ssd, axis=1).reshape(
        bp, len(new_ssd), 2, SSM_HEADS, SSM_HEAD_DIM, SSM_STATE)
    return (y_ctx.reshape(bp, seq, d), y_lat.reshape(db, dec_seq, d), new_state,
            jnp.concatenate(new_ckv, axis=1), jnp.concatenate(new_kr, axis=1))
```

```python
import functools

import jax
import jax.numpy as jnp
from jax import lax
from jax.experimental import pallas as pl
from jax.experimental.pallas import tpu as pltpu

F32 = jnp.float32
BF16 = jnp.bfloat16

EPS = 1e-6
N_MOD = 9
LANES = 128
A_GROUPS = 4
A_CHUNK = 128
SSM_HEADS = 8
SSM_HEAD_DIM = 64
SSM_GROUPS = 2
SSM_STATE = 128
SSM_CHUNK = 128
MLA_HEADS = 8
MLA_NOPE = 64
MLA_ROPE = 32
MLA_V = 64
CONV_K = 31
GRID_W = 64
ROPE_THETA = 10000.0

VMEM_LIMIT = 60 * 1024 * 1024


def _dot(a, b):
    return jnp.dot(a, b, preferred_element_type=F32)


def _dot_nt(a, b):
    return lax.dot_general(a, b, (((1,), (1,)), ((), ())), preferred_element_type=F32)


def _dot_tn(a, b):
    return lax.dot_general(a, b, (((0,), (0,)), ((), ())), preferred_element_type=F32)


def _rms(x, g):
    return x * lax.rsqrt(jnp.mean(x * x, axis=-1, keepdims=True) + EPS) * g


def _sigmoid(x):
    return 1.0 / (1.0 + jnp.exp(-x))


def _silu(x):
    return x * _sigmoid(x)


def _mod_index(tile_rows, n_ctx_rows, dec_seq):
    def index(i):
        return jnp.maximum((i * tile_rows - n_ctx_rows) // dec_seq + 1, 0)
    return index


def _resident(shape):
    nd = len(shape)
    return pl.BlockSpec(shape, lambda i: (0,) * nd, pipeline_mode=pl.Buffered(1))


def _mod_kernel(ct_ref, w_ref, b_ref, o_ref, *, n_sets):
    w = w_ref[0]
    s = _silu(ct_ref[...])
    rows = [jnp.sum(w * s[:, r:r + 1], axis=0, keepdims=True) for r in range(n_sets)]
    o_ref[0] = jnp.concatenate(rows, axis=0) + b_ref[0]


def _mod_call(cond_t, w_mod, b_mod):
    depth, d, n = w_mod.shape
    n_sets = cond_t.shape[1]
    tn = n // 4
    return pl.pallas_call(
        functools.partial(_mod_kernel, n_sets=n_sets),
        out_shape=jax.ShapeDtypeStruct((depth, n_sets, n), F32),
        grid=(depth, n // tn),
        in_specs=[
            pl.BlockSpec((d, n_sets), lambda l, j: (0, 0)),
            pl.BlockSpec((1, d, tn), lambda l, j: (l, 0, j)),
            pl.BlockSpec((1, 1, tn), lambda l, j: (l, 0, j)),
        ],
        out_specs=pl.BlockSpec((1, n_sets, tn), lambda l, j: (l, 0, j)),
        compiler_params=pltpu.CompilerParams(
            dimension_semantics=("arbitrary", "arbitrary"), vmem_limit_bytes=VMEM_LIMIT),
        name="adaln_mod",
    )(cond_t, w_mod, b_mod.reshape(depth, 1, n))


FFN_CHUNK = 2 * LANES


def _ffn_kernel(*refs, mrow, d_ff, ctx_tiles, first, final, layer, which):
    refs = list(refs)
    n_x = 2 if first else 1
    x_refs, (mod_ref, g_ref, wgu_hbm, wd_hbm), refs = refs[:n_x], refs[n_x:n_x + 4], refs[n_x + 4:]
    wgu_ref, wd_ref, sem = refs[-3:]
    i = pl.program_id(0)

    def weight_copies(c0):
        cols, rows = pl.ds(c0, FFN_CHUNK), pl.ds(c0, FFN_CHUNK)
        ucols = pl.ds(d_ff + c0, FFN_CHUNK)
        k = c0 // FFN_CHUNK
        return (pltpu.make_async_copy(wgu_hbm.at[layer, which, :, cols], wgu_ref.at[:, cols], sem.at[0, k]),
                pltpu.make_async_copy(wgu_hbm.at[layer, which, :, ucols], wgu_ref.at[:, ucols], sem.at[1, k]),
                pltpu.make_async_copy(wd_hbm.at[layer, which, rows, :], wd_ref.at[rows, :], sem.at[2, k]))

    def run(first_step):
        if first and first_step:
            x = x_refs[0][...]
        elif first:
            x = jnp.where(i < ctx_tiles, x_refs[0][...], x_refs[1][...])
        else:
            x = x_refs[0][...]
        m = mod_ref[0]
        shift, scale, gate = m[mrow:mrow + 1], m[mrow + 1:mrow + 2], m[mrow + 2:mrow + 3]
        h = (_rms(x, g_ref[...]) * (1.0 + scale) + shift).astype(BF16)
        acc = jnp.zeros(x.shape, F32)
        for c0 in range(0, d_ff, FFN_CHUNK):
            if first_step:
                for cp in weight_copies(c0):
                    cp.wait()
            g = _dot(h, wgu_ref[:, c0:c0 + FFN_CHUNK].astype(BF16))
            u = _dot(h, wgu_ref[:, d_ff + c0:d_ff + c0 + FFN_CHUNK].astype(BF16))
            a = (_silu(g) * u).astype(BF16)
            acc = acc + _dot(a, wd_ref[c0:c0 + FFN_CHUNK, :].astype(BF16))
        out = x + 0.5 * gate * acc
        if not final:
            refs[0][...] = out
            return
        gf_ref, yc_ref, yl_ref = refs[:3]
        y = _rms(out, gf_ref[...])
        if first_step:
            yc_ref[...] = y
            return

        @pl.when(i < ctx_tiles)
        def _():
            yc_ref[...] = y

        @pl.when(i >= ctx_tiles)
        def _():
            yl_ref[...] = y

    @pl.when(i == 0)
    def _():
        for c0 in range(0, d_ff, FFN_CHUNK):
            for cp in weight_copies(c0):
                cp.start()
        run(True)

    @pl.when(i > 0)
    def _():
        run(False)


def _ffn_call(xs, mod_l, g, w_gu, w_d, g_final, *, layer, which, mrow, tm, n_ctx_rows, dec_seq):
    first = len(xs) == 2
    final = g_final is not None
    d = xs[0].shape[1]
    t = sum(a.shape[0] for a in xs)
    d_ff = w_d.shape[2]
    ctx_tiles = n_ctx_rows // tm

    def ctx_map(i):
        return (jnp.minimum(i, ctx_tiles - 1), 0)

    def lat_map(i):
        return (jnp.maximum(i - ctx_tiles, 0), 0)

    x_specs = ([pl.BlockSpec((tm, d), ctx_map), pl.BlockSpec((tm, d), lat_map)] if first
               else [pl.BlockSpec((tm, d), lambda i: (i, 0))])
    in_specs = x_specs + [
        pl.BlockSpec((1, N_MOD, d), lambda i: (_mod_index(tm, n_ctx_rows, dec_seq)(i), 0, 0)),
        _resident((1, d)),
        pl.BlockSpec(memory_space=pl.ANY),
        pl.BlockSpec(memory_space=pl.ANY),
    ]
    scratch = [pltpu.VMEM(w_gu.shape[2:], F32), pltpu.VMEM(w_d.shape[2:], F32),
               pltpu.SemaphoreType.DMA((3, d_ff // FFN_CHUNK))]
    args = list(xs) + [mod_l, g.reshape(1, d), w_gu, w_d]
    if final:
        in_specs.append(_resident((1, d)))
        args.append(g_final.reshape(1, d))
        out_shape = (jax.ShapeDtypeStruct((n_ctx_rows, d), F32), jax.ShapeDtypeStruct((t - n_ctx_rows, d), F32))
        out_specs = (pl.BlockSpec((tm, d), ctx_map), pl.BlockSpec((tm, d), lat_map))
    else:
        out_shape = jax.ShapeDtypeStruct((t, d), F32)
        out_specs = pl.BlockSpec((tm, d), lambda i: (i, 0))
    return pl.pallas_call(
        functools.partial(_ffn_kernel, mrow=mrow, d_ff=d_ff, ctx_tiles=ctx_tiles, first=first, final=final,
                          layer=layer, which=which),
        out_shape=out_shape,
        grid=(t // tm,),
        in_specs=in_specs,
        out_specs=out_specs,
        scratch_shapes=scratch,
        compiler_params=pltpu.CompilerParams(
            dimension_semantics=("arbitrary",), vmem_limit_bytes=VMEM_LIMIT),
        name="ffn_final" if final else ("ffn_first" if first else "ffn"),
    )(*args)


A_WIDTH = A_GROUPS * LANES
SSM_INNER = SSM_HEADS * SSM_HEAD_DIM
SSM_BC = SSM_GROUPS * SSM_STATE
EV_UV = 0
EV_Z = EV_UV + 2 * A_WIDTH
EV_X = EV_Z + SSM_INNER
EV_B = EV_X + SSM_INNER
EV_C = EV_B + SSM_BC
EV_DT = EV_C + SSM_BC
EV_COLS = EV_DT + LANES


def _gelu_tanh(x):
    return 0.5 * x * (1.0 + jnp.tanh(0.7978845608028654 * (x + 0.044715 * (x * x * x))))


def _softplus(x):
    return jnp.maximum(x, 0.0) + jnp.log1p(jnp.exp(-jnp.abs(x)))


def _split3(v):
    hi = v.astype(BF16)
    r = v - hi.astype(F32)
    mid = r.astype(BF16)
    lo = (r - mid.astype(F32)).astype(BF16)
    return hi, mid, lo


def _expand(parts, e):
    n = parts[0].shape[0]
    y = _dot(jnp.concatenate(parts, axis=0), e)
    out = y[0:n]
    for k in range(1, len(parts)):
        out = out + y[k * n:(k + 1) * n]
    return out


def _even_kernel(x_ref, mod_ref, g_ref, win_ref, wout_ref, ws_ref, bs_ref, gv_ref, wc_ref, bc_ref,
                 dtb_ref, alog_ref, dsk_ref, go_ref, e_ref, *rest, seq_len, n_seq, has_h0, emit_state):
    rest = list(rest)
    h0_ref = rest.pop(0) if has_h0 else None
    o_ref = rest.pop(0)
    st_ref = rest.pop(0) if emit_state else None
    h_ref, mix_ref, xs_ref, b_ref, c_ref, dt_ref, cum_ref, y_ref, state_ref = rest

    rows = seq_len * n_seq
    ck = SSM_CHUNK
    n_chunk = seq_len // ck
    m = mod_ref[0]
    h_ref[...] = (_rms(x_ref[...], g_ref[...]) * (1.0 + m[4:5]) + m[3:4]).astype(BF16)

    vb = _rms(_gelu_tanh(_dot(h_ref[...], win_ref[:, EV_UV + A_WIDTH:EV_Z])), gv_ref[...]).astype(BF16)
    for g in range(A_GROUPS):
        l0 = g * LANES
        u = _gelu_tanh(_dot(h_ref[...], win_ref[:, EV_UV + l0:EV_UV + l0 + LANES]))
        for cc in range(rows // A_CHUNK):
            r0 = cc * A_CHUNK
            s = _dot(ws_ref[g], vb[r0:r0 + A_CHUNK, l0:l0 + LANES]) + bs_ref[g]
            mix_ref[r0:r0 + A_CHUNK, l0:l0 + LANES] = (u[r0:r0 + A_CHUNK] * s).astype(BF16)

    pw = 2 * LANES
    rin = lax.broadcasted_iota(jnp.int32, (rows, pw), 0) % seq_len
    for c0 in range(0, EV_DT - EV_X, pw):
        xbc = _dot(h_ref[...], win_ref[:, EV_X + c0:EV_X + c0 + pw])
        wc = wc_ref[:, c0:c0 + pw]
        prev = jnp.where(rin >= 1, pltpu.roll(xbc, 1, 0), 0.0)
        nxt = jnp.where(rin < seq_len - 1, pltpu.roll(xbc, rows - 1, 0), 0.0)
        xbc = _silu(prev * wc[0:1] + xbc * wc[1:2] + nxt * wc[2:3] + bc_ref[:, c0:c0 + pw])
        if c0 < SSM_INNER:
            xs_ref[:, c0:c0 + pw] = xbc
        elif c0 < SSM_INNER + SSM_BC:
            b_ref[:, c0 - SSM_INNER:c0 - SSM_INNER + pw] = xbc.astype(BF16)
        else:
            c_ref[:, c0 - SSM_INNER - SSM_BC:c0 - SSM_INNER - SSM_BC + pw] = xbc.astype(BF16)

    dt = _softplus(_dot(h_ref[...], win_ref[:, EV_DT:EV_COLS]) + dtb_ref[...])
    lane = lax.broadcasted_iota(jnp.int32, dt.shape, 1)
    rck = lax.broadcasted_iota(jnp.int32, dt.shape, 0) % ck
    da = jnp.where(lane < 2 * SSM_HEADS, dt * -jnp.exp(alog_ref[...]), 0.0)
    cf = da
    cb = da
    sh = 1
    while sh < ck:
        cf = cf + jnp.where(rck >= sh, pltpu.roll(cf, sh, 0), 0.0)
        cb = cb + jnp.where(rck < ck - sh, pltpu.roll(cb, rows - sh, 0), 0.0)
        sh *= 2
    dt_ref[...] = dt
    cum_ref[...] = jnp.where(lane < SSM_HEADS, cf, cb)

    ii = lax.broadcasted_iota(jnp.int32, (ck, ck), 0)
    jj = lax.broadcasted_iota(jnp.int32, (ck, ck), 1)
    lane_lo = lax.broadcasted_iota(jnp.int32, (ck, LANES), 1) < SSM_HEAD_DIM
    gw = SSM_INNER // SSM_GROUPS
    hpg = SSM_HEADS // SSM_GROUPS

    def chunk_step(r0, direction, slot):
        e = e_ref[:, direction * SSM_INNER:(direction + 1) * SSM_INNER]
        cum_n = cum_ref[pl.ds(r0, ck), :]
        cum_t = cum_n.T
        dt_n = dt_ref[pl.ds(r0, ck), :]
        dt_e = jnp.concatenate(
            [jnp.where(lane_lo,
                       jnp.broadcast_to(dt_n[:, direction * SSM_HEADS + hd:direction * SSM_HEADS + hd + 1], (ck, LANES)),
                       jnp.broadcast_to(dt_n[:, direction * SSM_HEADS + hd + 1:direction * SSM_HEADS + hd + 2], (ck, LANES)))
             for hd in range(0, SSM_HEADS, 2)], axis=1)
        cum_e = _expand(_split3(cum_n), e)
        x_c = xs_ref[pl.ds(r0, ck), :]
        xdt = x_c * dt_e
        if direction == 0:
            tot = cum_e[ck - 1:ck, :]
            tri = ii >= jj
        else:
            tot = cum_e[0:1, :]
            tri = ii <= jj
        ec = jnp.exp(cum_e)
        xd = (xdt * jnp.exp(tot - cum_e)).astype(BF16)
        bm = b_ref[pl.ds(r0, ck), :]
        cm = c_ref[pl.ds(r0, ck), :]
        st = state_ref[slot]
        y_parts = []
        st_parts = []
        for g in range(SSM_GROUPS):
            bg = bm[:, g * SSM_STATE:(g + 1) * SSM_STATE]
            cg = cm[:, g * SSM_STATE:(g + 1) * SSM_STATE]
            cbm = _dot_nt(cg, bg)
            for pr in range(hpg // 2):
                hd = g * hpg + 2 * pr
                l0 = hd * SSM_HEAD_DIM
                sc = []
                for k in range(2):
                    col = direction * SSM_HEADS + hd + k
                    seg = cum_n[:, col:col + 1] - cum_t[col:col + 1, :]
                    sc.append((cbm * jnp.where(tri, jnp.exp(seg), 0.0)).astype(BF16))
                xp = xdt[:, l0:l0 + LANES]
                rhs = jnp.concatenate([jnp.where(lane_lo, xp, 0.0), jnp.where(lane_lo, 0.0, xp)],
                                      axis=0).astype(BF16)
                y_parts.append(_dot(jnp.concatenate(sc, axis=1), rhs))
            y_off = _dot(cg, st[:, g * gw:(g + 1) * gw].astype(BF16))
            y_parts[-2] = y_parts[-2] + y_off[:, 0:LANES] * ec[:, g * gw:g * gw + LANES]
            y_parts[-1] = y_parts[-1] + y_off[:, LANES:gw] * ec[:, g * gw + LANES:(g + 1) * gw]
            st_parts.append(_dot_tn(bg, xd[:, g * gw:(g + 1) * gw]))
        y_ref[direction, pl.ds(r0, ck), :] = jnp.concatenate(y_parts, axis=1)
        state_ref[slot] = st * jnp.exp(tot) + jnp.concatenate(st_parts, axis=1)

    for s in range(n_seq):
        for direction in range(2):
            if has_h0:
                state_ref[2 * s + direction] = h0_ref[s, 0, direction].T
            else:
                state_ref[2 * s + direction] = jnp.zeros(state_ref.shape[1:], F32)

    def scan_body(k, carry):
        for s in range(n_seq):
            for direction in range(2):
                c = k if direction == 0 else n_chunk - 1 - k
                chunk_step(pl.multiple_of(s * seq_len + c * ck, ck), direction, 2 * s + direction)
        return carry

    lax.fori_loop(0, n_chunk, scan_body, 0)
    if emit_state:
        for s in range(n_seq):
            for direction in range(2):
                st_ref[s, 0, direction] = state_ref[2 * s + direction].T

    z = _dot(h_ref[...], win_ref[:, EV_Z:EV_X])
    y = y_ref[0] + y_ref[1] + xs_ref[...] * (dsk_ref[0:1] + dsk_ref[1:2])
    mix_ref[:, A_WIDTH:] = _rms(y * _silu(z), go_ref[...]).astype(BF16)
    for c0 in range(0, o_ref.shape[1], pw):
        o_ref[:, c0:c0 + pw] = (x_ref[:, c0:c0 + pw]
                                + m[5:6, c0:c0 + pw] * _dot(mix_ref[...], wout_ref[:, c0:c0 + pw]))


def _even_call(x, mod_l, g, p, h0, *, layer_idx, seq_len, n_seq, row0, n_tiles, mod_set, batch):
    t, d = x.shape
    rows = seq_len * n_seq
    tile0 = row0 // rows
    has_h0 = h0 is not None
    emit_state = not has_h0
    hp, ns = SSM_INNER, SSM_STATE
    in_specs = [
        pl.BlockSpec((rows, d), lambda i: (tile0 + i, 0)),
        pl.BlockSpec((1, N_MOD, d), lambda i: (mod_set(i), 0, 0)),
        _resident((1, d)),
        _resident(p["w_in"].shape),
        _resident(p["w_out"].shape),
        _resident(p["w_s"].shape),
        _resident(p["b_s"].shape),
        _resident(p["g_v"].shape),
        _resident(p["w_conv"].shape),
        _resident(p["b_conv"].shape),
        _resident(p["dt_bias"].shape),
        _resident(p["a_log"].shape),
        _resident(p["d_skip"].shape),
        _resident(p["g_out"].shape),
        _resident(p["expand"].shape),
    ]
    args = [x, mod_l, g.reshape(1, d), p["w_in"], p["w_out"], p["w_s"], p["b_s"], p["g_v"], p["w_conv"],
            p["b_conv"], p["dt_bias"], p["a_log"], p["d_skip"], p["g_out"], p["expand"]]
    out_shape = [jax.ShapeDtypeStruct((t, d), F32)]
    out_specs = [pl.BlockSpec((rows, d), lambda i: (tile0 + i, 0))]
    if has_h0:
        in_specs.append(pl.BlockSpec((n_seq, 1, 2, hp, ns), lambda i: (i, layer_idx, 0, 0, 0)))
        args.append(h0)
    if emit_state:
        out_shape.append(jax.ShapeDtypeStruct((batch, 1, 2, hp, ns), F32))
        out_specs.append(pl.BlockSpec((n_seq, 1, 2, hp, ns), lambda i: (i, 0, 0, 0, 0)))
    scratch = [
        pltpu.VMEM((rows, d), BF16),
        pltpu.VMEM((rows, A_WIDTH + SSM_INNER), BF16),
        pltpu.VMEM((rows, SSM_INNER), F32),
        pltpu.VMEM((rows, SSM_BC), BF16),
        pltpu.VMEM((rows, SSM_BC), BF16),
        pltpu.VMEM((rows, LANES), F32),
        pltpu.VMEM((rows, LANES), F32),
        pltpu.VMEM((2, rows, SSM_INNER), F32),
        pltpu.VMEM((2 * n_seq, SSM_STATE, SSM_INNER), F32),
    ]
    res = pl.pallas_call(
        functools.partial(_even_kernel, seq_len=seq_len, n_seq=n_seq, has_h0=has_h0, emit_state=emit_state),
        out_shape=out_shape,
        grid=(n_tiles,),
        in_specs=in_specs,
        out_specs=out_specs,
        scratch_shapes=scratch,
        input_output_aliases={0: 0},
        compiler_params=pltpu.CompilerParams(
            dimension_semantics=("arbitrary",), vmem_limit_bytes=VMEM_LIMIT),
        name="even_mixer_latent" if has_h0 else "even_mixer_context",
    )(*args)
    return res


def _pack_even(w_in, w_out, w_s, b_s, g_v, w_conv, b_conv, dt_bias, a_log, d_skip, g_out):
    d = w_in.shape[0]
    n_dt = 2 * SSM_HEADS
    w_in_p = jnp.concatenate([w_in[:, :EV_DT], w_in[:, EV_DT:], jnp.zeros((d, LANES - n_dt), F32)], axis=1)

    def narrow(v):
        return jnp.concatenate([v.reshape(1, n_dt), jnp.zeros((1, LANES - n_dt), F32)], axis=1)

    lane_head = jnp.arange(2 * SSM_INNER) // SSM_HEAD_DIM
    expand = (jnp.arange(LANES)[:, None] == lane_head[None, :]).astype(BF16)
    return {
        "w_in": w_in_p.astype(BF16),
        "w_out": w_out.astype(BF16),
        "w_s": w_s.astype(BF16),
        "b_s": jnp.broadcast_to(b_s[:, :, None], b_s.shape + (LANES,)),
        "g_v": g_v.reshape(1, -1),
        "w_conv": w_conv,
        "b_conv": b_conv.reshape(1, -1),
        "dt_bias": narrow(dt_bias),
        "a_log": narrow(a_log),
        "d_skip": jnp.repeat(d_skip, SSM_HEAD_DIM, axis=1),
        "g_out": g_out.reshape(1, -1),
        "expand": expand,
    }


MLA_Q_RANK = 384
MLA_KV_RANK = 256
CONV_WIDTH = 512
HEAD_W = LANES
ROPE_L0 = MLA_NOPE
OD_CQ = 0
OD_CKV = OD_CQ + MLA_Q_RANK
OD_A = OD_CKV + MLA_KV_RANK
OD_KR = OD_A + 2 * CONV_WIDTH
OD_KRS = OD_KR + HEAD_W
OD_COLS = OD_KRS + HEAD_W
CONV_PAD = 16
CONV_RB = 64
ATT_QB = 256


def _odd_kernel(x_ref, mod_ref, g_ref, win_ref, wuq_ref, wukv_ref, wout_ref, gcq_ref, gckv_ref,
                wdw_ref, bdw_ref, gln_ref, bln_ref, *rest, seq_len, n_seq, past):
    rest = list(rest)
    has_cache = past > 0
    if has_cache:
        cckv_ref, ckr_ref, cos_ref, sin_ref = rest[:4]
        o_ref = rest[4]
        rest = rest[5:]
    else:
        o_ref, ckv_out_ref, kr_out_ref = rest[:3]
        rest = rest[3:]
    h_ref, cq_ref, ckv_ref, kr_ref, mix_ref, q_ref, k_ref, v_ref, oh_ref, pad_ref = rest

    nh = MLA_HEADS
    hw = nh * HEAD_W
    n_keys = past + seq_len
    scale = (MLA_NOPE + MLA_ROPE) ** -0.5
    m = mod_ref[0]
    h_ref[...] = (_rms(x_ref[...], g_ref[...]) * (1.0 + m[4:5]) + m[3:4]).astype(BF16)

    cq_ref[...] = _rms(_dot(h_ref[...], win_ref[:, OD_CQ:OD_CKV]), gcq_ref[...]).astype(BF16)
    ckv = _rms(_dot(h_ref[...], win_ref[:, OD_CKV:OD_A]), gckv_ref[...])
    ckv_ref[...] = ckv.astype(BF16)
    kr = _dot(h_ref[...], win_ref[:, OD_KR:OD_KRS])
    if has_cache:
        cos = jnp.concatenate([cos_ref[...]] * n_seq, axis=0)
        sin = jnp.concatenate([sin_ref[...]] * n_seq, axis=0)
        kr = kr * cos + _dot(h_ref[...], win_ref[:, OD_KRS:OD_COLS]) * sin
    else:
        for s in range(n_seq):
            ckv_out_ref[s, 0] = ckv[s * seq_len:(s + 1) * seq_len]
            kr_out_ref[s, 0] = kr[s * seq_len:(s + 1) * seq_len, ROPE_L0:ROPE_L0 + MLA_ROPE]
    kr_ref[...] = kr

    n_pair = nh // 2

    def twice(a):
        return jnp.concatenate([a, a], axis=1)

    def q_body(pr, carry):
        q = _dot(cq_ref[...], wuq_ref[pr])
        if has_cache:
            q = q * twice(cos) + _dot(cq_ref[...], wuq_ref[n_pair + pr]) * twice(sin)
        q_ref[pr] = q.astype(BF16)
        return carry

    lax.fori_loop(0, n_pair, q_body, 0, unroll=2)

    n_qb = seq_len // ATT_QB
    for s in range(n_seq):
        r0 = s * seq_len

        def kv_body(pr, carry, s=s, r0=r0):
            lat = ckv_ref[r0:r0 + seq_len]
            k_ref[pr, past:n_keys] = (_dot(lat, wukv_ref[pr]) + twice(kr_ref[r0:r0 + seq_len])).astype(BF16)
            v_ref[pr, past:n_keys] = _dot(lat, wukv_ref[n_pair + pr]).astype(BF16)
            if has_cache:
                lat_c = cckv_ref[s, 0].astype(BF16)
                k_ref[pr, 0:past] = (_dot(lat_c, wukv_ref[pr]) + twice(ckr_ref[s, 0])).astype(BF16)
                v_ref[pr, 0:past] = _dot(lat_c, wukv_ref[n_pair + pr]).astype(BF16)
            return carry

        lax.fori_loop(0, n_pair, kv_body, 0, unroll=2)

        def att_body(it, carry, r0=r0):
            pr = it // n_qb
            q0 = pl.multiple_of(r0 + (it % n_qb) * ATT_QB, ATT_QB)
            outs = []
            for l0 in range(0, 2 * HEAD_W, HEAD_W):
                sc = _dot_nt(q_ref[pr, pl.ds(q0, ATT_QB), l0:l0 + HEAD_W], k_ref[pr, :, l0:l0 + HEAD_W]) * scale
                p = jnp.exp(sc - jnp.max(sc, axis=-1, keepdims=True))
                den = jnp.sum(p, axis=-1, keepdims=True)
                outs.append((_dot(p.astype(BF16), v_ref[pr, :, l0:l0 + HEAD_W]) / den).astype(BF16))
            oh_ref[pr, pl.ds(q0, ATT_QB), :] = jnp.concatenate(outs, axis=1)
            return carry

        lax.fori_loop(0, n_pair * n_qb, att_body, 0, unroll=4)

    for pr in range(n_pair):
        mix_ref[:, pr * 2 * HEAD_W:(pr + 1) * 2 * HEAD_W] = oh_ref[pr]

    pw = 2 * LANES
    zpad = jnp.zeros((CONV_PAD, CONV_WIDTH), F32)
    span = CONV_RB + 2 * CONV_PAD
    for s in range(n_seq):
        r0 = s * seq_len
        pad_ref[0:CONV_PAD] = zpad
        pad_ref[CONV_PAD + seq_len:2 * CONV_PAD + seq_len] = zpad
        for c0 in range(0, CONV_WIDTH, pw):
            a = _dot(h_ref[r0:r0 + seq_len], win_ref[:, OD_A + c0:OD_A + c0 + pw])
            gt = _dot(h_ref[r0:r0 + seq_len], win_ref[:, OD_A + CONV_WIDTH + c0:OD_A + CONV_WIDTH + c0 + pw])
            pad_ref[CONV_PAD:CONV_PAD + seq_len, c0:c0 + pw] = a * _sigmoid(gt)

        def conv_body(b, carry, r0=r0):
            base = pl.multiple_of(b * CONV_RB, CONV_RB)
            parts = []
            for l0 in range(0, CONV_WIDTH, LANES):
                blk = pad_ref[pl.ds(base, span), l0:l0 + LANES]
                acc = jnp.broadcast_to(bdw_ref[:, l0:l0 + LANES], (CONV_RB, LANES))
                for r in range(8):
                    sh = blk if r == 0 else pltpu.roll(blk, span - r, 0)
                    for qq in range((span - CONV_RB) // 8):
                        k = 8 * qq + r - 1
                        if 0 <= k < CONV_K:
                            acc = acc + wdw_ref[k:k + 1, l0:l0 + LANES] * sh[8 * qq:8 * qq + CONV_RB]
                parts.append(acc)
            acc = jnp.concatenate(parts, axis=1)
            dc = acc - jnp.mean(acc, axis=-1, keepdims=True)
            var = jnp.mean(dc * dc, axis=-1, keepdims=True)
            dn = dc * lax.rsqrt(var + EPS) * gln_ref[...] + bln_ref[...]
            mix_ref[pl.ds(r0 + base, CONV_RB), hw:] = _silu(dn).astype(BF16)
            return carry

        lax.fori_loop(0, seq_len // CONV_RB, conv_body, 0)

    for c0 in range(0, o_ref.shape[1], pw):
        o_ref[:, c0:c0 + pw] = (x_ref[:, c0:c0 + pw]
                                + m[5:6, c0:c0 + pw] * _dot(mix_ref[...], wout_ref[:, c0:c0 + pw]))


def _odd_call(x, mod_l, g, p, cache, *, layer_idx, seq_len, n_seq, row0, n_tiles, mod_set, batch):
    t, d = x.shape
    rows = seq_len * n_seq
    tile0 = row0 // rows
    has_cache = cache is not None
    past = cache[0].shape[2] if has_cache else 0
    names = ["w_in", "w_uq", "w_ukv", "w_out", "g_cq", "g_ckv", "w_dw", "b_dw", "g_ln", "b_ln"]
    in_specs = [
        pl.BlockSpec((rows, d), lambda i: (tile0 + i, 0)),
        pl.BlockSpec((1, N_MOD, d), lambda i: (mod_set(i), 0, 0)),
        _resident((1, d)),
    ] + [_resident(p[k].shape) for k in names]
    args = [x, mod_l, g.reshape(1, d)] + [p[k] for k in names]
    out_shape = [jax.ShapeDtypeStruct((t, d), F32)]
    out_specs = [pl.BlockSpec((rows, d), lambda i: (tile0 + i, 0))]
    if has_cache:
        cckv, ckr, cos, sin = cache
        in_specs += [
            pl.BlockSpec((n_seq, 1, past, MLA_KV_RANK), lambda i: (i, layer_idx, 0, 0)),
            pl.BlockSpec((n_seq, 1, past, HEAD_W), lambda i: (i, layer_idx, 0, 0)),
            _resident(cos.shape),
            _resident(sin.shape),
        ]
        args += [cckv, ckr, cos, sin]
    else:
        out_shape += [jax.ShapeDtypeStruct((batch, 1, seq_len, MLA_KV_RANK), F32),
                      jax.ShapeDtypeStruct((batch, 1, seq_len, MLA_ROPE), F32)]
        out_specs += [pl.BlockSpec((n_seq, 1, seq_len, MLA_KV_RANK), lambda i: (i, 0, 0, 0)),
                      pl.BlockSpec((n_seq, 1, seq_len, MLA_ROPE), lambda i: (i, 0, 0, 0))]
    n_keys = past + seq_len
    hw = MLA_HEADS * HEAD_W
    scratch = [
        pltpu.VMEM((rows, d), BF16),
        pltpu.VMEM((rows, MLA_Q_RANK), BF16),
        pltpu.VMEM((rows, MLA_KV_RANK), BF16),
        pltpu.VMEM((rows, HEAD_W), F32),
        pltpu.VMEM((rows, hw + CONV_WIDTH), BF16),
        pltpu.VMEM((MLA_HEADS // 2, rows, 2 * HEAD_W), BF16),
        pltpu.VMEM((MLA_HEADS // 2, n_keys, 2 * HEAD_W), BF16),
        pltpu.VMEM((MLA_HEADS // 2, n_keys, 2 * HEAD_W), BF16),
        pltpu.VMEM((MLA_HEADS // 2, rows, 2 * HEAD_W), BF16),
        pltpu.VMEM((seq_len + 2 * CONV_PAD, CONV_WIDTH), F32),
    ]
    return pl.pallas_call(
        functools.partial(_odd_kernel, seq_len=seq_len, n_seq=n_seq, past=past),
        out_shape=out_shape,
        grid=(n_tiles,),
        in_specs=in_specs,
        out_specs=out_specs,
        scratch_shapes=scratch,
        input_output_aliases={0: 0},
        compiler_params=pltpu.CompilerParams(
            dimension_semantics=("arbitrary",), vmem_limit_bytes=VMEM_LIMIT),
        name="odd_mixer_latent" if has_cache else "odd_mixer_context",
    )(*args)


def _head_tiles(w, per_head, start, width, lane0):
    k = w.shape[0]
    blk = w.reshape(k, MLA_HEADS, per_head)[:, :, start:start + width]
    out = jnp.zeros((k, MLA_HEADS, HEAD_W), w.dtype).at[:, :, lane0:lane0 + width].set(blk)
    return out.reshape(k, MLA_HEADS // 2, 2 * HEAD_W).transpose(1, 0, 2)


def _rope_swap(w):
    q = MLA_ROPE // 4
    idx = jnp.concatenate([jnp.arange(q, 2 * q), jnp.arange(0, q), jnp.arange(3 * q, 4 * q), jnp.arange(2 * q, 3 * q)])
    return w[..., idx]


def _pack_odd(w_in, w_out, g_cq, w_uq, g_ckv, w_ukv, w_dw, b_dw, g_ln, b_ln):
    d = w_in.shape[0]
    o1 = MLA_Q_RANK
    o2 = o1 + MLA_KV_RANK
    o3 = o2 + MLA_ROPE
    w_kr = w_in[:, o2:o3]

    def kr_tile(w):
        return jnp.zeros((d, HEAD_W), F32).at[:, ROPE_L0:ROPE_L0 + MLA_ROPE].set(w)

    w_in_p = jnp.concatenate([w_in[:, :o2], w_in[:, o3:], kr_tile(w_kr), kr_tile(_rope_swap(w_kr))], axis=1)
    qh = MLA_NOPE + MLA_ROPE
    rank = w_uq.shape[0]
    w_q3 = w_uq.reshape(rank, MLA_HEADS, qh)
    w_q_sw = jnp.concatenate([jnp.zeros((rank, MLA_HEADS, MLA_NOPE), F32), _rope_swap(w_q3[:, :, MLA_NOPE:])],
                             axis=2).reshape(rank, MLA_HEADS * qh)
    w_uq_p = jnp.concatenate([_head_tiles(w_uq, qh, 0, qh, 0),
                              _head_tiles(w_q_sw, qh, MLA_NOPE, MLA_ROPE, ROPE_L0)], axis=0)
    kvh = MLA_NOPE + MLA_V
    w_ukv_p = jnp.concatenate([_head_tiles(w_ukv, kvh, 0, MLA_NOPE, 0),
                               _head_tiles(w_ukv, kvh, MLA_NOPE, MLA_V, 0)], axis=0)
    n_att = MLA_HEADS * MLA_V
    w_att = jnp.zeros((MLA_HEADS, HEAD_W, d), F32).at[:, :MLA_V, :].set(w_out[:n_att].reshape(MLA_HEADS, MLA_V, d))
    w_out_p = jnp.concatenate([w_att.reshape(MLA_HEADS * HEAD_W, d), w_out[n_att:]], axis=0)
    return {
        "w_in": w_in_p.astype(BF16),
        "w_uq": w_uq_p.astype(BF16),
        "w_ukv": w_ukv_p.astype(BF16),
        "w_out": w_out_p.astype(BF16),
        "g_cq": g_cq.reshape(1, -1),
        "g_ckv": g_ckv.reshape(1, -1),
        "w_dw": w_dw,
        "b_dw": b_dw.reshape(1, -1),
        "g_ln": g_ln.reshape(1, -1),
        "b_ln": b_ln.reshape(1, -1),
    }


def _rope_tables(length):
    n_freq = MLA_ROPE // 4
    pos = jnp.arange(length)
    row = (pos // GRID_W).astype(F32)
    col = (pos % GRID_W).astype(F32)
    freqs = ROPE_THETA ** (-jnp.arange(n_freq, dtype=F32) / n_freq)
    ar = row[:, None] * freqs
    ac = col[:, None] * freqs
    cos = jnp.concatenate([jnp.cos(ar), jnp.cos(ar), jnp.cos(ac), jnp.cos(ac)], axis=1)
    sin = jnp.concatenate([-jnp.sin(ar), jnp.sin(ar), -jnp.sin(ac), jnp.sin(ac)], axis=1)
    tail = HEAD_W - ROPE_L0 - MLA_ROPE
    cos_t = jnp.concatenate([jnp.ones((length, ROPE_L0), F32), cos, jnp.zeros((length, tail), F32)], axis=1)
    sin_t = jnp.concatenate([jnp.zeros((length, ROPE_L0), F32), sin, jnp.zeros((length, tail), F32)], axis=1)
    return cos_t, sin_t


def kernel(x_prompt, x_sample, state_ssd, cache_mla_ckv, cache_mla_krope, c, c_ctx, w_mod, b_mod, g_norm, w_ff_gu, w_ff_down, w_in_even, w_out_even, w_spatial, b_spatial, g_gmlp_v, w_conv_ssm, b_conv_ssm, dt_bias, a_log, d_skip, g_ssm_out, w_in_odd, w_out_odd, g_cq, w_uq, g_ckv, w_ukv, w_dwconv, b_dwconv, g_conv_ln, b_conv_ln, g_final):
    bp, seq, d = x_prompt.shape
    db, dec_seq, _ = x_sample.shape
    n_ctx = bp * seq
    cond_t = jnp.concatenate([c_ctx[None], c], axis=0).T
    mod = _mod_call(cond_t, w_mod, b_mod).reshape(w_mod.shape[0], 1 + db, N_MOD, d)
    depth = w_mod.shape[0]
    ffn = functools.partial(_ffn_call, tm=512, n_ctx_rows=n_ctx, dec_seq=dec_seq)
    ctx_seqs = 4
    ctx_tile = dict(seq_len=seq, n_seq=ctx_seqs, row0=0, n_tiles=bp // ctx_seqs, mod_set=lambda i: 0, batch=bp)
    lat_tile = dict(seq_len=dec_seq, n_seq=1, row0=n_ctx, n_tiles=db, mod_set=lambda i: 1 + i, batch=bp)
    h0 = state_ssd.reshape(db, state_ssd.shape[1], 2, SSM_INNER, SSM_STATE)
    ckr_tile = jnp.pad(cache_mla_krope, ((0, 0), (0, 0), (0, 0), (ROPE_L0, HEAD_W - ROPE_L0 - MLA_ROPE)))
    cos_t, sin_t = _rope_tables(dec_seq)
    new_ssd, new_ckv, new_kr = [], [], []
    xs = [x_prompt.reshape(n_ctx, d), x_sample.reshape(db * dec_seq, d)]
    for l in range(depth):
        i = l // 2
        last = l == depth - 1
        x = ffn(xs, mod[l], g_norm[l, 0], w_ff_gu, w_ff_down, None, layer=l, which=0, mrow=0)
        if l % 2 == 0:
            pe = _pack_even(w_in_even[i], w_out_even[i], w_spatial[i], b_spatial[i], g_gmlp_v[i], w_conv_ssm[i],
                            b_conv_ssm[i], dt_bias[i], a_log[i], d_skip[i], g_ssm_out[i])
            x, st = _even_call(x, mod[l], g_norm[l, 1], pe, None, layer_idx=i, **ctx_tile)
            (x,) = _even_call(x, mod[l], g_norm[l, 1], pe, h0, layer_idx=i, **lat_tile)
            new_ssd.append(st)
        else:
            po = _pack_odd(w_in_odd[i], w_out_odd[i], g_cq[i], w_uq[i], g_ckv[i], w_ukv[i], w_dwconv[i],
                           b_dwconv[i], g_conv_ln[i], b_conv_ln[i])
            x, ckv, kr = _odd_call(x, mod[l], g_norm[l, 1], po, None, layer_idx=i, **ctx_tile)
            (x,) = _odd_call(x, mod[l], g_norm[l, 1], po, (cache_mla_ckv, ckr_tile, cos_t, sin_t),
                             layer_idx=i, **lat_tile)
            new_ckv.append(ckv)
            new_kr.append(kr)
        res = ffn([x], mod[l], g_norm[l, 2], w_ff_gu, w_ff_down, g_final if last else None,
                  layer=l, which=1, mrow=6)
        xs = [res]
    y_ctx, y_lat = res
    new_state = jnp.concatenate(new_ssd, axis=1).reshape(
        bp, len(new_ssd), 2, SSM_HEADS, SSM_HEAD_DIM, SSM_STATE)
    return (y_ctx.reshape(bp, seq, d), y_lat.reshape(db, dec_seq, d), new_state,
            jnp.concatenate(new_ckv, axis=1), jnp.concatenate(new_kr, axis=1))
```

```python
import functools

import jax
import jax.numpy as jnp
from jax import lax
from jax.experimental import pallas as pl
from jax.experimental.pallas import tpu as pltpu

F32 = jnp.float32
BF16 = jnp.bfloat16

EPS = 1e-6
N_MOD = 9
LANES = 128
A_GROUPS = 4
A_CHUNK = 128
SSM_HEADS = 8
SSM_HEAD_DIM = 64
SSM_GROUPS = 2
SSM_STATE = 128
SSM_CHUNK = 128
MLA_HEADS = 8
MLA_NOPE = 64
MLA_ROPE = 32
MLA_V = 64
CONV_K = 31
GRID_W = 64
ROPE_THETA = 10000.0

VMEM_LIMIT = 60 * 1024 * 1024


def _dot(a, b):
    return jnp.dot(a, b, preferred_element_type=F32)


def _dot_nt(a, b):
    return lax.dot_general(a, b, (((1,), (1,)), ((), ())), preferred_element_type=F32)


def _dot_tn(a, b):
    return lax.dot_general(a, b, (((0,), (0,)), ((), ())), preferred_element_type=F32)


def _rms(x, g):
    return x * lax.rsqrt(jnp.mean(x * x, axis=-1, keepdims=True) + EPS) * g


def _sigmoid(x):
    return 1.0 / (1.0 + jnp.exp(-x))


def _silu(x):
    return x * _sigmoid(x)


def _mod_index(tile_rows, n_ctx_rows, dec_seq):
    def index(i):
        return jnp.maximum((i * tile_rows - n_ctx_rows) // dec_seq + 1, 0)
    return index


def _mod_spec(mod_all, layer, set_of_tile):
    return pl.BlockSpec((1, 1) + mod_all.shape[2:], lambda i: (layer, set_of_tile(i), 0, 0))


def _gain_spec(g_all, idx):
    return pl.BlockSpec((1,) + g_all.shape[1:], lambda i: (idx, 0, 0), pipeline_mode=pl.Buffered(1))


def _resident(shape):
    nd = len(shape)
    return pl.BlockSpec(shape, lambda i: (0,) * nd, pipeline_mode=pl.Buffered(1))


def _mod_kernel(ct_ref, w_ref, b_ref, o_ref, *, n_sets):
    w = w_ref[0]
    s = _silu(ct_ref[...])
    rows = [jnp.sum(w * s[:, r:r + 1], axis=0, keepdims=True) for r in range(n_sets)]
    o_ref[0] = jnp.concatenate(rows, axis=0) + b_ref[0]


def _mod_call(cond_t, w_mod, b_mod):
    depth, d, n = w_mod.shape
    n_sets = cond_t.shape[1]
    tn = n // 4
    return pl.pallas_call(
        functools.partial(_mod_kernel, n_sets=n_sets),
        out_shape=jax.ShapeDtypeStruct((depth, n_sets, n), F32),
        grid=(depth, n // tn),
        in_specs=[
            pl.BlockSpec((d, n_sets), lambda l, j: (0, 0)),
            pl.BlockSpec((1, d, tn), lambda l, j: (l, 0, j)),
            pl.BlockSpec((1, 1, tn), lambda l, j: (l, 0, j)),
        ],
        out_specs=pl.BlockSpec((1, n_sets, tn), lambda l, j: (l, 0, j)),
        compiler_params=pltpu.CompilerParams(
            dimension_semantics=("arbitrary", "arbitrary"), vmem_limit_bytes=VMEM_LIMIT),
        name="adaln_mod",
    )(cond_t, w_mod, b_mod.reshape(depth, 1, n))


FFN_CHUNK = 2 * LANES


def _ffn_kernel(*refs, mrow, d_ff, ctx_tiles, first, final, layer, which):
    refs = list(refs)
    n_x = 2 if first else 1
    x_refs, (mod_ref, g_ref, wgu_hbm, wd_hbm), refs = refs[:n_x], refs[n_x:n_x + 4], refs[n_x + 4:]
    wgu_ref, wd_ref, sem = refs[-3:]
    i = pl.program_id(0)

    def weight_copies(c0):
        cols, rows = pl.ds(c0, FFN_CHUNK), pl.ds(c0, FFN_CHUNK)
        ucols = pl.ds(d_ff + c0, FFN_CHUNK)
        k = c0 // FFN_CHUNK
        return (pltpu.make_async_copy(wgu_hbm.at[layer, which, :, cols], wgu_ref.at[:, cols], sem.at[0, k]),
                pltpu.make_async_copy(wgu_hbm.at[layer, which, :, ucols], wgu_ref.at[:, ucols], sem.at[1, k]),
                pltpu.make_async_copy(wd_hbm.at[layer, which, rows, :], wd_ref.at[rows, :], sem.at[2, k]))

    def run(first_step):
        if first and first_step:
            x = x_refs[0][...]
        elif first:
            x = jnp.where(i < ctx_tiles, x_refs[0][...], x_refs[1][...])
        else:
            x = x_refs[0][...]
        m = mod_ref[0, 0]
        shift, scale, gate = m[mrow:mrow + 1], m[mrow + 1:mrow + 2], m[mrow + 2:mrow + 3]
        h = (_rms(x, g_ref[0]) * (1.0 + scale) + shift).astype(BF16)
        acc = jnp.zeros(x.shape, F32)
        for c0 in range(0, d_ff, FFN_CHUNK):
            if first_step:
                for cp in weight_copies(c0):
                    cp.wait()
            g = _dot(h, wgu_ref[:, c0:c0 + FFN_CHUNK].astype(BF16))
            u = _dot(h, wgu_ref[:, d_ff + c0:d_ff + c0 + FFN_CHUNK].astype(BF16))
            a = (_silu(g) * u).astype(BF16)
            acc = acc + _dot(a, wd_ref[c0:c0 + FFN_CHUNK, :].astype(BF16))
        out = x + 0.5 * gate * acc
        if not final:
            refs[0][...] = out
            return
        gf_ref, yc_ref, yl_ref = refs[:3]
        y = _rms(out, gf_ref[...])
        if first_step:
            yc_ref[...] = y
            return

        @pl.when(i < ctx_tiles)
        def _():
            yc_ref[...] = y

        @pl.when(i >= ctx_tiles)
        def _():
            yl_ref[...] = y

    @pl.when(i == 0)
    def _():
        for c0 in range(0, d_ff, FFN_CHUNK):
            for cp in weight_copies(c0):
                cp.start()
        run(True)

    @pl.when(i > 0)
    def _():
        run(False)


def _ffn_call(xs, mod_l, g, w_gu, w_d, g_final, *, layer, which, mrow, tm, n_ctx_rows, dec_seq):
    first = len(xs) == 2
    final = g_final is not None
    d = xs[0].shape[1]
    t = sum(a.shape[0] for a in xs)
    d_ff = w_d.shape[2]
    ctx_tiles = n_ctx_rows // tm

    def ctx_map(i):
        return (jnp.minimum(i, ctx_tiles - 1), 0)

    def lat_map(i):
        return (jnp.maximum(i - ctx_tiles, 0), 0)

    x_specs = ([pl.BlockSpec((tm, d), ctx_map), pl.BlockSpec((tm, d), lat_map)] if first
               else [pl.BlockSpec((tm, d), lambda i: (i, 0))])
    in_specs = x_specs + [
        _mod_spec(mod_l, layer, _mod_index(tm, n_ctx_rows, dec_seq)),
        _gain_spec(*g),
        pl.BlockSpec(memory_space=pl.ANY),
        pl.BlockSpec(memory_space=pl.ANY),
    ]
    scratch = [pltpu.VMEM(w_gu.shape[2:], F32), pltpu.VMEM(w_d.shape[2:], F32),
               pltpu.SemaphoreType.DMA((3, d_ff // FFN_CHUNK))]
    args = list(xs) + [mod_l, g[0], w_gu, w_d]
    if final:
        in_specs.append(_resident((1, d)))
        args.append(g_final.reshape(1, d))
        out_shape = (jax.ShapeDtypeStruct((n_ctx_rows, d), F32), jax.ShapeDtypeStruct((t - n_ctx_rows, d), F32))
        out_specs = (pl.BlockSpec((tm, d), ctx_map), pl.BlockSpec((tm, d), lat_map))
    else:
        out_shape = jax.ShapeDtypeStruct((t, d), F32)
        out_specs = pl.BlockSpec((tm, d), lambda i: (i, 0))
    return pl.pallas_call(
        functools.partial(_ffn_kernel, mrow=mrow, d_ff=d_ff, ctx_tiles=ctx_tiles, first=first, final=final,
                          layer=layer, which=which),
        out_shape=out_shape,
        grid=(t // tm,),
        in_specs=in_specs,
        out_specs=out_specs,
        scratch_shapes=scratch,
        compiler_params=pltpu.CompilerParams(
            dimension_semantics=("arbitrary",), vmem_limit_bytes=VMEM_LIMIT),
        name="ffn_final" if final else ("ffn_first" if first else "ffn"),
    )(*args)


A_WIDTH = A_GROUPS * LANES
SSM_INNER = SSM_HEADS * SSM_HEAD_DIM
SSM_BC = SSM_GROUPS * SSM_STATE
EV_UV = 0
EV_Z = EV_UV + 2 * A_WIDTH
EV_X = EV_Z + SSM_INNER
EV_B = EV_X + SSM_INNER
EV_C = EV_B + SSM_BC
EV_DT = EV_C + SSM_BC


def _gelu_tanh(x):
    return 0.5 * x * (1.0 + jnp.tanh(0.7978845608028654 * (x + 0.044715 * (x * x * x))))


def _softplus(x):
    return jnp.maximum(x, 0.0) + jnp.log1p(jnp.exp(-jnp.abs(x)))


def _split3(v):
    hi = v.astype(BF16)
    r = v - hi.astype(F32)
    mid = r.astype(BF16)
    lo = (r - mid.astype(F32)).astype(BF16)
    return hi, mid, lo


def _expand(parts, e):
    n = parts[0].shape[0]
    y = _dot(jnp.concatenate(parts, axis=0), e)
    out = y[0:n]
    for k in range(1, len(parts)):
        out = out + y[k * n:(k + 1) * n]
    return out


def _even_kernel(x_ref, mod_ref, g_ref, win_ref, wdt_ref, wout_ref, ws_ref, bs_ref, gv_ref, wc_ref, bc_ref,
                 dtb_ref, alog_ref, dsk_ref, go_ref, e_ref, *rest, seq_len, n_seq, has_h0, emit_state,
                 n_carry, out_slot):
    rest = list(rest)
    h0_ref = rest.pop(0) if has_h0 else None
    rest = rest[n_carry:]
    o_ref = rest.pop(0)
    st_ref = rest.pop(0) if emit_state else None
    h_ref, mix_ref, xs_ref, b_ref, c_ref, dt_ref, cum_ref, y_ref, state_ref = rest

    rows = seq_len * n_seq
    ck = SSM_CHUNK
    n_chunk = seq_len // ck
    m = mod_ref[0, 0]
    h_ref[...] = (_rms(x_ref[...], g_ref[0]) * (1.0 + m[4:5]) + m[3:4]).astype(BF16)

    vb = _rms(_gelu_tanh(_dot(h_ref[...], win_ref[:, EV_UV + A_WIDTH:EV_Z])), gv_ref[...]).astype(BF16)
    for g in range(A_GROUPS):
        l0 = g * LANES
        u = _gelu_tanh(_dot(h_ref[...], win_ref[:, EV_UV + l0:EV_UV + l0 + LANES]))
        for cc in range(rows // A_CHUNK):
            r0 = cc * A_CHUNK
            s = _dot(ws_ref[g], vb[r0:r0 + A_CHUNK, l0:l0 + LANES]) + bs_ref[g]
            mix_ref[r0:r0 + A_CHUNK, l0:l0 + LANES] = (u[r0:r0 + A_CHUNK] * s).astype(BF16)

    pw = 2 * LANES
    rin = lax.broadcasted_iota(jnp.int32, (rows, pw), 0) % seq_len
    for c0 in range(0, EV_DT - EV_X, pw):
        xbc = _dot(h_ref[...], win_ref[:, EV_X + c0:EV_X + c0 + pw])
        wc = wc_ref[:, c0:c0 + pw]
        prev = jnp.where(rin >= 1, pltpu.roll(xbc, 1, 0), 0.0)
        nxt = jnp.where(rin < seq_len - 1, pltpu.roll(xbc, rows - 1, 0), 0.0)
        xbc = _silu(prev * wc[0:1] + xbc * wc[1:2] + nxt * wc[2:3] + bc_ref[:, c0:c0 + pw])
        if c0 < SSM_INNER:
            xs_ref[:, c0:c0 + pw] = xbc
        elif c0 < SSM_INNER + SSM_BC:
            b_ref[:, c0 - SSM_INNER:c0 - SSM_INNER + pw] = xbc.astype(BF16)
        else:
            c_ref[:, c0 - SSM_INNER - SSM_BC:c0 - SSM_INNER - SSM_BC + pw] = xbc.astype(BF16)

    dt = _softplus(_dot(h_ref[...], wdt_ref[...]) + dtb_ref[...])
    lane = lax.broadcasted_iota(jnp.int32, dt.shape, 1)
    rck = lax.broadcasted_iota(jnp.int32, dt.shape, 0) % ck
    da = jnp.where(lane < 2 * SSM_HEADS, dt * -jnp.exp(alog_ref[...]), 0.0)
    cf = da
    cb = da
    sh = 1
    while sh < ck:
        cf = cf + jnp.where(rck >= sh, pltpu.roll(cf, sh, 0), 0.0)
        cb = cb + jnp.where(rck < ck - sh, pltpu.roll(cb, rows - sh, 0), 0.0)
        sh *= 2
    dt_ref[...] = dt
    cum_ref[...] = jnp.where(lane < SSM_HEADS, cf, cb)

    ii = lax.broadcasted_iota(jnp.int32, (ck, ck), 0)
    jj = lax.broadcasted_iota(jnp.int32, (ck, ck), 1)
    lane_lo = lax.broadcasted_iota(jnp.int32, (ck, LANES), 1) < SSM_HEAD_DIM
    gw = SSM_INNER // SSM_GROUPS
    hpg = SSM_HEADS // SSM_GROUPS

    def chunk_step(r0, direction, slot):
        e = e_ref[:, direction * SSM_INNER:(direction + 1) * SSM_INNER]
        cum_n = cum_ref[pl.ds(r0, ck), :]
        cum_t = cum_n.T
        dt_n = dt_ref[pl.ds(r0, ck), :]
        dt_e = jnp.concatenate(
            [jnp.where(lane_lo,
                       jnp.broadcast_to(dt_n[:, direction * SSM_HEADS + hd:direction * SSM_HEADS + hd + 1], (ck, LANES)),
                       jnp.broadcast_to(dt_n[:, direction * SSM_HEADS + hd + 1:direction * SSM_HEADS + hd + 2], (ck, LANES)))
             for hd in range(0, SSM_HEADS, 2)], axis=1)
        cum_e = _expand(_split3(cum_n), e)
        x_c = xs_ref[pl.ds(r0, ck), :]
        xdt = x_c * dt_e
        if direction == 0:
            tot = cum_e[ck - 1:ck, :]
            tri = ii >= jj
        else:
            tot = cum_e[0:1, :]
            tri = ii <= jj
        ec = jnp.exp(cum_e)
        xd = (xdt * jnp.exp(tot - cum_e)).astype(BF16)
        bm = b_ref[pl.ds(r0, ck), :]
        cm = c_ref[pl.ds(r0, ck), :]
        st = state_ref[slot]
        y_parts = []
        st_parts = []
        for g in range(SSM_GROUPS):
            bg = bm[:, g * SSM_STATE:(g + 1) * SSM_STATE]
            cg = cm[:, g * SSM_STATE:(g + 1) * SSM_STATE]
            cbm = _dot_nt(cg, bg)
            for pr in range(hpg // 2):
                hd = g * hpg + 2 * pr
                l0 = hd * SSM_HEAD_DIM
                sc = []
                for k in range(2):
                    col = direction * SSM_HEADS + hd + k
                    seg = cum_n[:, col:col + 1] - cum_t[col:col + 1, :]
                    sc.append((cbm * jnp.where(tri, jnp.exp(seg), 0.0)).astype(BF16))
                xp = xdt[:, l0:l0 + LANES]
                rhs = jnp.concatenate([jnp.where(lane_lo, xp, 0.0), jnp.where(lane_lo, 0.0, xp)],
                                      axis=0).astype(BF16)
                y_parts.append(_dot(jnp.concatenate(sc, axis=1), rhs))
            y_off = _dot(cg, st[:, g * gw:(g + 1) * gw].astype(BF16))
            y_parts[-2] = y_parts[-2] + y_off[:, 0:LANES] * ec[:, g * gw:g * gw + LANES]
            y_parts[-1] = y_parts[-1] + y_off[:, LANES:gw] * ec[:, g * gw + LANES:(g + 1) * gw]
            st_parts.append(_dot_tn(bg, xd[:, g * gw:(g + 1) * gw]))
        y_ref[direction, pl.ds(r0, ck), :] = jnp.concatenate(y_parts, axis=1)
        state_ref[slot] = st * jnp.exp(tot) + jnp.concatenate(st_parts, axis=1)

    for s in range(n_seq):
        for direction in range(2):
            if has_h0:
                state_ref[2 * s + direction] = h0_ref[s, 0, direction].T
            else:
                state_ref[2 * s + direction] = jnp.zeros(state_ref.shape[1:], F32)

    def scan_body(k, carry):
        for s in range(n_seq):
            for direction in range(2):
                c = k if direction == 0 else n_chunk - 1 - k
                chunk_step(pl.multiple_of(s * seq_len + c * ck, ck), direction, 2 * s + direction)
        return carry

    lax.fori_loop(0, n_chunk, scan_body, 0)
    if emit_state:
        for s in range(n_seq):
            for direction in range(2):
                st_ref[s, out_slot, direction] = state_ref[2 * s + direction].T
        for slot in range(st_ref.shape[1]):
            if slot != out_slot:
                st_ref[:, slot] = jnp.zeros((n_seq,) + st_ref.shape[2:], F32)

    z = _dot(h_ref[...], win_ref[:, EV_Z:EV_X])
    y = y_ref[0] + y_ref[1] + xs_ref[...] * (dsk_ref[0:1] + dsk_ref[1:2])
    mix_ref[:, A_WIDTH:] = _rms(y * _silu(z), go_ref[...]).astype(BF16)
    for c0 in range(0, o_ref.shape[1], pw):
        o_ref[:, c0:c0 + pw] = (x_ref[:, c0:c0 + pw]
                                + m[5:6, c0:c0 + pw] * _dot(mix_ref[...], wout_ref[:, c0:c0 + pw]))


def _even_call(x, mod_l, g, p, h0, *, layer, layer_idx, seq_len, n_seq, row0, n_tiles, mod_set, batch,
               n_layers=1, carry=None):
    t, d = x.shape
    rows = seq_len * n_seq
    tile0 = row0 // rows
    has_h0 = h0 is not None
    emit_state = not has_h0
    hp, ns = SSM_INNER, SSM_STATE
    in_specs = [
        pl.BlockSpec((rows, d), lambda i: (tile0 + i, 0)),
        _mod_spec(mod_l, layer, mod_set),
        _gain_spec(*g),
        _resident(p["w_in"].shape),
        _resident(p["w_dt"].shape),
        _resident(p["w_out"].shape),
        _resident(p["w_s"].shape),
        _resident(p["b_s"].shape),
        _resident(p["g_v"].shape),
        _resident(p["w_conv"].shape),
        _resident(p["b_conv"].shape),
        _resident(p["dt_bias"].shape),
        _resident(p["a_log"].shape),
        _resident(p["d_skip"].shape),
        _resident(p["g_out"].shape),
        _resident(p["expand"].shape),
    ]
    args = [x, mod_l, g[0], p["w_in"], p["w_dt"], p["w_out"], p["w_s"], p["b_s"], p["g_v"], p["w_conv"],
            p["b_conv"], p["dt_bias"], p["a_log"], p["d_skip"], p["g_out"], p["expand"]]
    out_shape = [jax.ShapeDtypeStruct((t, d), F32)]
    out_specs = [pl.BlockSpec((rows, d), lambda i: (tile0 + i, 0))]
    if has_h0:
        in_specs.append(pl.BlockSpec((n_seq, 1, 2, hp, ns), lambda i: (i, layer_idx, 0, 0, 0)))
        args.append(h0)
    aliases = {0: 0}
    if emit_state:
        out_shape.append(jax.ShapeDtypeStruct((batch, n_layers, 2, hp, ns), F32))
        if carry is None:
            out_specs.append(pl.BlockSpec((n_seq, n_layers, 2, hp, ns), lambda i: (i, 0, 0, 0, 0)))
        else:
            out_specs.append(pl.BlockSpec((n_seq, 1, 2, hp, ns), lambda i: (i, layer_idx, 0, 0, 0)))
            in_specs.append(pl.BlockSpec(memory_space=pl.ANY))
            args.append(carry)
            aliases[len(args) - 1] = 1
    out_slot = layer_idx if carry is None else 0
    scratch = [
        pltpu.VMEM((rows, d), BF16),
        pltpu.VMEM((rows, A_WIDTH + SSM_INNER), BF16),
        pltpu.VMEM((rows, SSM_INNER), F32),
        pltpu.VMEM((rows, SSM_BC), BF16),
        pltpu.VMEM((rows, SSM_BC), BF16),
        pltpu.VMEM((rows, LANES), F32),
        pltpu.VMEM((rows, LANES), F32),
        pltpu.VMEM((2, rows, SSM_INNER), F32),
        pltpu.VMEM((2 * n_seq, SSM_STATE, SSM_INNER), F32),
    ]
    res = pl.pallas_call(
        functools.partial(_even_kernel, seq_len=seq_len, n_seq=n_seq, has_h0=has_h0, emit_state=emit_state,
                          n_carry=len(aliases) - 1, out_slot=out_slot),
        out_shape=out_shape,
        grid=(n_tiles,),
        in_specs=in_specs,
        out_specs=out_specs,
        scratch_shapes=scratch,
        input_output_aliases=aliases,
        compiler_params=pltpu.CompilerParams(
            dimension_semantics=("arbitrary",), vmem_limit_bytes=VMEM_LIMIT),
        name="even_mixer_latent" if has_h0 else "even_mixer_context",
    )(*args)
    return res


def _pack_even(w_in, w_out, w_s, b_s, g_v, w_conv, b_conv, dt_bias, a_log, d_skip, g_out):
    d = w_in.shape[0]
    n_dt = 2 * SSM_HEADS
    w_dt = jnp.concatenate([w_in[:, EV_DT:], jnp.zeros((d, LANES - n_dt), F32)], axis=1)

    def narrow(v):
        return jnp.concatenate([v.reshape(1, n_dt), jnp.zeros((1, LANES - n_dt), F32)], axis=1)

    lane_head = jnp.arange(2 * SSM_INNER) // SSM_HEAD_DIM
    expand = (jnp.arange(LANES)[:, None] == lane_head[None, :]).astype(BF16)
    return {
        "w_in": w_in[:, :EV_DT].astype(BF16),
        "w_dt": w_dt.astype(BF16),
        "w_out": w_out.astype(BF16),
        "w_s": w_s.astype(BF16),
        "b_s": jnp.broadcast_to(b_s[:, :, None], b_s.shape + (LANES,)),
        "g_v": g_v.reshape(1, -1),
        "w_conv": w_conv,
        "b_conv": b_conv.reshape(1, -1),
        "dt_bias": narrow(dt_bias),
        "a_log": narrow(a_log),
        "d_skip": jnp.repeat(d_skip, SSM_HEAD_DIM, axis=1),
        "g_out": g_out.reshape(1, -1),
        "expand": expand,
    }


MLA_Q_RANK = 384
MLA_KV_RANK = 256
CONV_WIDTH = 512
HEAD_W = LANES
ROPE_L0 = MLA_NOPE
OD_CQ = 0
OD_CKV = OD_CQ + MLA_Q_RANK
OD_LAT = OD_CKV + MLA_KV_RANK
CONV_PAD = 16
CONV_RB = 64
ATT_QB = 256


def _odd_kernel(x_ref, mod_ref, g_ref, win_ref, wag_ref, wkr_ref, wuq_ref, wuk_ref, wuv_ref, wout_ref, gcq_ref, gckv_ref,
                wdw_ref, bdw_ref, gln_ref, bln_ref, *rest, seq_len, n_seq, past, n_carry, out_slot):
    rest = list(rest)
    has_cache = past > 0
    if has_cache:
        cckv_ref, ckr_ref, cos_ref, sin_ref = rest[:4]
        o_ref = rest[4]
        rest = rest[5:]
    else:
        rest = rest[n_carry:]
        o_ref, ckv_out_ref, kr_out_ref = rest[:3]
        rest = rest[3:]
    h_ref, cq_ref, ckv_ref, kr_ref, mix_ref, q_ref, k_ref, v_ref, oh_ref, pad_ref = rest

    nh = MLA_HEADS
    hw = nh * MLA_V
    first_head_lanes = lax.broadcasted_iota(jnp.int32, (ATT_QB, HEAD_W), 1) < MLA_V
    n_keys = past + seq_len
    scale = (MLA_NOPE + MLA_ROPE) ** -0.5
    m = mod_ref[0, 0]
    h_ref[...] = (_rms(x_ref[...], g_ref[0]) * (1.0 + m[4:5]) + m[3:4]).astype(BF16)

    cq_ref[...] = _rms(_dot(h_ref[...], win_ref[:, OD_CQ:OD_CKV]), gcq_ref[...]).astype(BF16)
    ckv = _rms(_dot(h_ref[...], win_ref[:, OD_CKV:OD_LAT]), gckv_ref[...])
    ckv_ref[...] = ckv.astype(BF16)
    kr = _dot(h_ref[...], wkr_ref[:, 0:HEAD_W])
    if has_cache:
        cos = jnp.concatenate([cos_ref[...]] * n_seq, axis=0)
        sin = jnp.concatenate([sin_ref[...]] * n_seq, axis=0)
        kr = kr * cos + _dot(h_ref[...], wkr_ref[:, HEAD_W:2 * HEAD_W]) * sin
    else:
        for s in range(n_seq):
            ckv_out_ref[s, out_slot] = ckv[s * seq_len:(s + 1) * seq_len]
            kr_out_ref[s, out_slot] = kr[s * seq_len:(s + 1) * seq_len, ROPE_L0:ROPE_L0 + MLA_ROPE]
        for slot in range(ckv_out_ref.shape[1]):
            if slot != out_slot:
                ckv_out_ref[:, slot] = jnp.zeros((n_seq,) + ckv_out_ref.shape[2:], F32)
                kr_out_ref[:, slot] = jnp.zeros((n_seq,) + kr_out_ref.shape[2:], F32)
    kr_ref[...] = kr

    n_pair = nh // 2

    def twice(a):
        return jnp.concatenate([a, a], axis=1)

    def q_body(pr, carry):
        q = _dot(cq_ref[...], wuq_ref[pr])
        if has_cache:
            q = q * twice(cos) + _dot(cq_ref[...], wuq_ref[n_pair + pr]) * twice(sin)
        q_ref[pr] = q.astype(BF16)
        return carry

    lax.fori_loop(0, n_pair, q_body, 0, unroll=2)

    n_qb = seq_len // ATT_QB
    for s in range(n_seq):
        r0 = s * seq_len

        def kv_body(pr, carry, s=s, r0=r0):
            lat = ckv_ref[r0:r0 + seq_len]
            k_ref[pr, past:n_keys] = (_dot(lat, wuk_ref[pr]) + twice(kr_ref[r0:r0 + seq_len])).astype(BF16)
            v_ref[pr, past:n_keys] = _dot(lat, wuv_ref[pr]).astype(BF16)
            if has_cache:
                lat_c = cckv_ref[s, 0].astype(BF16)
                k_ref[pr, 0:past] = (_dot(lat_c, wuk_ref[pr]) + twice(ckr_ref[s, 0])).astype(BF16)
                v_ref[pr, 0:past] = _dot(lat_c, wuv_ref[pr]).astype(BF16)
            return carry

        lax.fori_loop(0, n_pair, kv_body, 0, unroll=2)

        def att_body(it, carry, r0=r0):
            pr = it // n_qb
            q0 = pl.multiple_of(r0 + (it % n_qb) * ATT_QB, ATT_QB)
            outs = []
            for l0 in range(0, 2 * HEAD_W, HEAD_W):
                sc = _dot_nt(q_ref[pr, pl.ds(q0, ATT_QB), l0:l0 + HEAD_W], k_ref[pr, :, l0:l0 + HEAD_W]) * scale
                p = jnp.exp(sc - jnp.max(sc, axis=-1, keepdims=True))
                den = jnp.sum(p, axis=-1, keepdims=True)
                outs.append(_dot(p.astype(BF16), v_ref[pr]) / den)
            oh_ref[pr, pl.ds(q0, ATT_QB), :] = jnp.where(first_head_lanes, outs[0], outs[1]).astype(BF16)
            return carry

        lax.fori_loop(0, n_pair * n_qb, att_body, 0, unroll=4)

    for pr in range(n_pair):
        mix_ref[:, pr * HEAD_W:(pr + 1) * HEAD_W] = oh_ref[pr]

    pw = 2 * LANES
    zpad = jnp.zeros((CONV_PAD, CONV_WIDTH), F32)
    span = CONV_RB + 2 * CONV_PAD
    for s in range(n_seq):
        r0 = s * seq_len
        pad_ref[0:CONV_PAD] = zpad
        pad_ref[CONV_PAD + seq_len:2 * CONV_PAD + seq_len] = zpad
        for c0 in range(0, CONV_WIDTH, pw):
            a = _dot(h_ref[r0:r0 + seq_len], wag_ref[:, c0:c0 + pw])
            gt = _dot(h_ref[r0:r0 + seq_len], wag_ref[:, CONV_WIDTH + c0:CONV_WIDTH + c0 + pw])
            pad_ref[CONV_PAD:CONV_PAD + seq_len, c0:c0 + pw] = a * _sigmoid(gt)

        def conv_body(b, carry, r0=r0):
            base = pl.multiple_of(b * CONV_RB, CONV_RB)
            parts = []
            for l0 in range(0, CONV_WIDTH, LANES):
                blk = pad_ref[pl.ds(base, span), l0:l0 + LANES]
                acc = jnp.broadcast_to(bdw_ref[:, l0:l0 + LANES], (CONV_RB, LANES))
                for r in range(8):
                    sh = blk if r == 0 else pltpu.roll(blk, span - r, 0)
                    for qq in range((span - CONV_RB) // 8):
                        k = 8 * qq + r - 1
                        if 0 <= k < CONV_K:
                            acc = acc + wdw_ref[k:k + 1, l0:l0 + LANES] * sh[8 * qq:8 * qq + CONV_RB]
                parts.append(acc)
            acc = jnp.concatenate(parts, axis=1)
            dc = acc - jnp.mean(acc, axis=-1, keepdims=True)
            var = jnp.mean(dc * dc, axis=-1, keepdims=True)
            dn = dc * lax.rsqrt(var + EPS) * gln_ref[...] + bln_ref[...]
            mix_ref[pl.ds(r0 + base, CONV_RB), hw:] = _silu(dn).astype(BF16)
            return carry

        lax.fori_loop(0, seq_len // CONV_RB, conv_body, 0)

    for c0 in range(0, o_ref.shape[1], pw):
        o_ref[:, c0:c0 + pw] = (x_ref[:, c0:c0 + pw]
                                + m[5:6, c0:c0 + pw] * _dot(mix_ref[...], wout_ref[:, c0:c0 + pw]))


def _odd_call(x, mod_l, g, p, cache, *, layer, layer_idx, seq_len, n_seq, row0, n_tiles, mod_set, batch,
              n_layers=1, carry=None):
    t, d = x.shape
    rows = seq_len * n_seq
    tile0 = row0 // rows
    has_cache = cache is not None
    past = cache[0].shape[2] if has_cache else 0
    names = ["w_in", "w_ag", "w_kr", "w_uq", "w_uk", "w_uv", "w_out", "g_cq", "g_ckv", "w_dw", "b_dw", "g_ln", "b_ln"]
    in_specs = [
        pl.BlockSpec((rows, d), lambda i: (tile0 + i, 0)),
        _mod_spec(mod_l, layer, mod_set),
        _gain_spec(*g),
    ] + [_resident(p[k].shape) for k in names]
    args = [x, mod_l, g[0]] + [p[k] for k in names]
    out_shape = [jax.ShapeDtypeStruct((t, d), F32)]
    out_specs = [pl.BlockSpec((rows, d), lambda i: (tile0 + i, 0))]
    aliases = {0: 0}
    if has_cache:
        cckv, ckr, cos, sin = cache
        in_specs += [
            pl.BlockSpec((n_seq, 1, past, MLA_KV_RANK), lambda i: (i, layer_idx, 0, 0)),
            pl.BlockSpec((n_seq, 1, past, HEAD_W), lambda i: (i, layer_idx, 0, 0)),
            _resident(cos.shape),
            _resident(sin.shape),
        ]
        args += [cckv, ckr, cos, sin]
    else:
        out_shape += [jax.ShapeDtypeStruct((batch, n_layers, seq_len, MLA_KV_RANK), F32),
                      jax.ShapeDtypeStruct((batch, n_layers, seq_len, MLA_ROPE), F32)]
        if carry is None:
            out_specs += [pl.BlockSpec((n_seq, n_layers, seq_len, MLA_KV_RANK), lambda i: (i, 0, 0, 0)),
                          pl.BlockSpec((n_seq, n_layers, seq_len, MLA_ROPE), lambda i: (i, 0, 0, 0))]
        else:
            out_specs += [pl.BlockSpec((n_seq, 1, seq_len, MLA_KV_RANK), lambda i: (i, layer_idx, 0, 0)),
                          pl.BlockSpec((n_seq, 1, seq_len, MLA_ROPE), lambda i: (i, layer_idx, 0, 0))]
            for k, arr in enumerate(carry):
                in_specs.append(pl.BlockSpec(memory_space=pl.ANY))
                args.append(arr)
                aliases[len(args) - 1] = 1 + k
    n_keys = past + seq_len
    scratch = [
        pltpu.VMEM((rows, d), BF16),
        pltpu.VMEM((rows, MLA_Q_RANK), BF16),
        pltpu.VMEM((rows, MLA_KV_RANK), BF16),
        pltpu.VMEM((rows, HEAD_W), F32),
        pltpu.VMEM((rows, MLA_HEADS * MLA_V + CONV_WIDTH), BF16),
        pltpu.VMEM((MLA_HEADS // 2, rows, 2 * HEAD_W), BF16),
        pltpu.VMEM((MLA_HEADS // 2, n_keys, 2 * HEAD_W), BF16),
        pltpu.VMEM((MLA_HEADS // 2, n_keys, 2 * MLA_V), BF16),
        pltpu.VMEM((MLA_HEADS // 2, rows, 2 * MLA_V), BF16),
        pltpu.VMEM((seq_len + 2 * CONV_PAD, CONV_WIDTH), F32),
    ]
    return pl.pallas_call(
        functools.partial(_odd_kernel, seq_len=seq_len, n_seq=n_seq, past=past, n_carry=len(aliases) - 1,
                          out_slot=layer_idx if (not has_cache and carry is None) else 0),
        out_shape=out_shape,
        grid=(n_tiles,),
        in_specs=in_specs,
        out_specs=out_specs,
        scratch_shapes=scratch,
        input_output_aliases=aliases,
        compiler_params=pltpu.CompilerParams(
            dimension_semantics=("arbitrary",), vmem_limit_bytes=VMEM_LIMIT),
        name="odd_mixer_latent" if has_cache else "odd_mixer_context",
    )(*args)


def _head_tiles(w, per_head, start, width, lane0):
    k = w.shape[0]
    blk = w.reshape(k, MLA_HEADS, per_head)[:, :, start:start + width]
    out = jnp.zeros((k, MLA_HEADS, HEAD_W), w.dtype).at[:, :, lane0:lane0 + width].set(blk)
    return out.reshape(k, MLA_HEADS // 2, 2 * HEAD_W).transpose(1, 0, 2)


def _rope_swap(w):
    q = MLA_ROPE // 4
    idx = jnp.concatenate([jnp.arange(q, 2 * q), jnp.arange(0, q), jnp.arange(3 * q, 4 * q), jnp.arange(2 * q, 3 * q)])
    return w[..., idx]


def _pack_odd(w_in, w_out, g_cq, w_uq, g_ckv, w_ukv, w_dw, b_dw, g_ln, b_ln):
    d = w_in.shape[0]
    o1 = MLA_Q_RANK
    o2 = o1 + MLA_KV_RANK
    o3 = o2 + MLA_ROPE
    w_kr = w_in[:, o2:o3]

    def kr_tile(w):
        return jnp.zeros((d, HEAD_W), F32).at[:, ROPE_L0:ROPE_L0 + MLA_ROPE].set(w)

    w_kr_p = jnp.concatenate([kr_tile(w_kr), kr_tile(_rope_swap(w_kr))], axis=1)
    qh = MLA_NOPE + MLA_ROPE
    rank = w_uq.shape[0]
    w_q3 = w_uq.reshape(rank, MLA_HEADS, qh)
    w_q_sw = jnp.concatenate([jnp.zeros((rank, MLA_HEADS, MLA_NOPE), F32), _rope_swap(w_q3[:, :, MLA_NOPE:])],
                             axis=2).reshape(rank, MLA_HEADS * qh)
    w_uq_p = jnp.concatenate([_head_tiles(w_uq, qh, 0, qh, 0),
                              _head_tiles(w_q_sw, qh, MLA_NOPE, MLA_ROPE, ROPE_L0)], axis=0)
    kvh = MLA_NOPE + MLA_V
    w_uk_p = _head_tiles(w_ukv, kvh, 0, MLA_NOPE, 0)
    rank_kv = w_ukv.shape[0]
    w_uv_p = (w_ukv.reshape(rank_kv, MLA_HEADS, kvh)[:, :, MLA_NOPE:]
              .reshape(rank_kv, MLA_HEADS // 2, 2 * MLA_V).transpose(1, 0, 2))
    return {
        "w_in": w_in[:, :o2].astype(BF16),
        "w_ag": w_in[:, o3:].astype(BF16),
        "w_kr": w_kr_p.astype(BF16),
        "w_uq": w_uq_p.astype(BF16),
        "w_uk": w_uk_p.astype(BF16),
        "w_uv": w_uv_p.astype(BF16),
        "w_out": w_out.astype(BF16),
        "g_cq": g_cq.reshape(1, -1),
        "g_ckv": g_ckv.reshape(1, -1),
        "w_dw": w_dw,
        "b_dw": b_dw.reshape(1, -1),
        "g_ln": g_ln.reshape(1, -1),
        "b_ln": b_ln.reshape(1, -1),
    }


def _rope_tables(length):
    n_freq = MLA_ROPE // 4
    pos = jnp.arange(length)
    row = (pos // GRID_W).astype(F32)
    col = (pos % GRID_W).astype(F32)
    freqs = ROPE_THETA ** (-jnp.arange(n_freq, dtype=F32) / n_freq)
    ar = row[:, None] * freqs
    ac = col[:, None] * freqs
    cos = jnp.concatenate([jnp.cos(ar), jnp.cos(ar), jnp.cos(ac), jnp.cos(ac)], axis=1)
    sin = jnp.concatenate([-jnp.sin(ar), jnp.sin(ar), -jnp.sin(ac), jnp.sin(ac)], axis=1)
    tail = HEAD_W - ROPE_L0 - MLA_ROPE
    cos_t = jnp.concatenate([jnp.ones((length, ROPE_L0), F32), cos, jnp.zeros((length, tail), F32)], axis=1)
    sin_t = jnp.concatenate([jnp.zeros((length, ROPE_L0), F32), sin, jnp.zeros((length, tail), F32)], axis=1)
    return cos_t, sin_t


def kernel(x_prompt, x_sample, state_ssd, cache_mla_ckv, cache_mla_krope, c, c_ctx, w_mod, b_mod, g_norm, w_ff_gu, w_ff_down, w_in_even, w_out_even, w_spatial, b_spatial, g_gmlp_v, w_conv_ssm, b_conv_ssm, dt_bias, a_log, d_skip, g_ssm_out, w_in_odd, w_out_odd, g_cq, w_uq, g_ckv, w_ukv, w_dwconv, b_dwconv, g_conv_ln, b_conv_ln, g_final):
    bp, seq, d = x_prompt.shape
    db, dec_seq, _ = x_sample.shape
    n_ctx = bp * seq
    cond_t = jnp.concatenate([c_ctx[None], c], axis=0).T
    mod = _mod_call(cond_t, w_mod, b_mod).reshape(w_mod.shape[0], 1 + db, N_MOD, d)
    depth = w_mod.shape[0]
    ffn = functools.partial(_ffn_call, tm=512, n_ctx_rows=n_ctx, dec_seq=dec_seq)
    ctx_seqs = 4
    ctx_tile = dict(seq_len=seq, n_seq=ctx_seqs, row0=0, n_tiles=bp // ctx_seqs, mod_set=lambda i: 0, batch=bp)
    lat_tile = dict(seq_len=dec_seq, n_seq=1, row0=n_ctx, n_tiles=db, mod_set=lambda i: 1 + i, batch=bp)
    h0 = state_ssd.reshape(db, state_ssd.shape[1], 2, SSM_INNER, SSM_STATE)
    ckr_tile = jnp.pad(cache_mla_krope, ((0, 0), (0, 0), (0, 0), (ROPE_L0, HEAD_W - ROPE_L0 - MLA_ROPE)))
    cos_t, sin_t = _rope_tables(dec_seq)
    n_even, n_odd = (depth + 1) // 2, depth // 2
    new_state = new_cache = None
    xs = [x_prompt.reshape(n_ctx, d), x_sample.reshape(db * dec_seq, d)]
    n_gain = g_norm.shape[1]
    gains = g_norm.reshape(depth * n_gain, 1, d)
    for l in range(depth):
        i = l // 2
        last = l == depth - 1
        x = ffn(xs, mod, (gains, l * n_gain), w_ff_gu, w_ff_down, None, layer=l, which=0, mrow=0)
        g_mix = (gains, l * n_gain + 1)
        if l % 2 == 0:
            pe = _pack_even(w_in_even[i], w_out_even[i], w_spatial[i], b_spatial[i], g_gmlp_v[i], w_conv_ssm[i],
                            b_conv_ssm[i], dt_bias[i], a_log[i], d_skip[i], g_ssm_out[i])
            x, new_state = _even_call(x, mod, g_mix, pe, None, layer=l, layer_idx=i, n_layers=n_even,
                                      carry=new_state, **ctx_tile)
            (x,) = _even_call(x, mod, g_mix, pe, h0, layer=l, layer_idx=i, **lat_tile)
        else:
            po = _pack_odd(w_in_odd[i], w_out_odd[i], g_cq[i], w_uq[i], g_ckv[i], w_ukv[i], w_dwconv[i],
                           b_dwconv[i], g_conv_ln[i], b_conv_ln[i])
            x, *new_cache = _odd_call(x, mod, g_mix, po, None, layer=l, layer_idx=i, n_layers=n_odd,
                                      carry=new_cache, **ctx_tile)
            (x,) = _odd_call(x, mod, g_mix, po, (cache_mla_ckv, ckr_tile, cos_t, sin_t),
                             layer=l, layer_idx=i, **lat_tile)
        res = ffn([x], mod, (gains, l * n_gain + 2), w_ff_gu, w_ff_down, g_final if last else None,
                  layer=l, which=1, mrow=6)
        xs = [res]
    y_ctx, y_lat = res
    new_state = new_state.reshape(bp, n_even, 2, SSM_HEADS, SSM_HEAD_DIM, SSM_STATE)
    return (y_ctx.reshape(bp, seq, d), y_lat.reshape(db, dec_seq, d), new_state, new_cache[0], new_cache[1])
```

```python
import functools

import jax
import jax.numpy as jnp
from jax import lax
from jax.experimental import pallas as pl
from jax.experimental.pallas import tpu as pltpu

F32 = jnp.float32
BF16 = jnp.bfloat16

EPS = 1e-6
N_MOD = 9
LANES = 128
A_GROUPS = 4
A_CHUNK = 128
SSM_HEADS = 8
SSM_HEAD_DIM = 64
SSM_GROUPS = 2
SSM_STATE = 128
SSM_CHUNK = 128
MLA_HEADS = 8
MLA_NOPE = 64
MLA_ROPE = 32
MLA_V = 64
CONV_K = 31
GRID_W = 64
ROPE_THETA = 10000.0

VMEM_LIMIT = 60 * 1024 * 1024


def _dot(a, b):
    return jnp.dot(a, b, preferred_element_type=F32)


def _dot_nt(a, b):
    return lax.dot_general(a, b, (((1,), (1,)), ((), ())), preferred_element_type=F32)


def _dot_tn(a, b):
    return lax.dot_general(a, b, (((0,), (0,)), ((), ())), preferred_element_type=F32)


def _rms(x, g):
    return x * lax.rsqrt(jnp.mean(x * x, axis=-1, keepdims=True) + EPS) * g


def _sigmoid(x):
    return 1.0 / (1.0 + jnp.exp(-x))


def _silu(x):
    return x * _sigmoid(x)


def _mod_index(tile_rows, n_ctx_rows, dec_seq):
    def index(i):
        return jnp.maximum((i * tile_rows - n_ctx_rows) // dec_seq + 1, 0)
    return index


def _mod_spec(mod_all, layer, set_of_tile):
    return pl.BlockSpec((1, 1) + mod_all.shape[2:], lambda i: (layer, set_of_tile(i), 0, 0))


def _gain_spec(g_all, idx):
    return pl.BlockSpec((1,) + g_all.shape[1:], lambda i: (idx, 0, 0), pipeline_mode=pl.Buffered(1))


def _resident(shape):
    nd = len(shape)
    return pl.BlockSpec(shape, lambda i: (0,) * nd, pipeline_mode=pl.Buffered(1))


def _mod_kernel(ct_ref, w_ref, b_ref, o_ref, *, n_sets):
    w = w_ref[0]
    s = _silu(ct_ref[...])
    rows = [jnp.sum(w * s[:, r:r + 1], axis=0, keepdims=True) for r in range(n_sets)]
    o_ref[0] = jnp.concatenate(rows, axis=0) + b_ref[0]


def _mod_call(cond_t, w_mod, b_mod):
    depth, d, n = w_mod.shape
    n_sets = cond_t.shape[1]
    tn = n // 4
    return pl.pallas_call(
        functools.partial(_mod_kernel, n_sets=n_sets),
        out_shape=jax.ShapeDtypeStruct((depth, n_sets, n), F32),
        grid=(depth, n // tn),
        in_specs=[
            pl.BlockSpec((d, n_sets), lambda l, j: (0, 0)),
            pl.BlockSpec((1, d, tn), lambda l, j: (l, 0, j)),
            pl.BlockSpec((1, 1, tn), lambda l, j: (l, 0, j)),
        ],
        out_specs=pl.BlockSpec((1, n_sets, tn), lambda l, j: (l, 0, j)),
        compiler_params=pltpu.CompilerParams(
            dimension_semantics=("arbitrary", "arbitrary"), vmem_limit_bytes=VMEM_LIMIT),
        name="adaln_mod",
    )(cond_t, w_mod, b_mod.reshape(depth, 1, n))


FFN_CHUNK = 2 * LANES


def _ffn_kernel(*refs, mrow, d_ff, ctx_tiles, first, final, layer, which):
    refs = list(refs)
    n_x = 2 if first else 1
    x_refs, (mod_ref, g_ref, wgu_hbm, wd_hbm), refs = refs[:n_x], refs[n_x:n_x + 4], refs[n_x + 4:]
    wgu_ref, wd_ref, sem = refs[-3:]
    i = pl.program_id(0)

    def weight_copies(c0):
        cols, rows = pl.ds(c0, FFN_CHUNK), pl.ds(c0, FFN_CHUNK)
        ucols = pl.ds(d_ff + c0, FFN_CHUNK)
        k = c0 // FFN_CHUNK
        return (pltpu.make_async_copy(wgu_hbm.at[layer, which, :, cols], wgu_ref.at[:, cols], sem.at[0, k]),
                pltpu.make_async_copy(wgu_hbm.at[layer, which, :, ucols], wgu_ref.at[:, ucols], sem.at[1, k]),
                pltpu.make_async_copy(wd_hbm.at[layer, which, rows, :], wd_ref.at[rows, :], sem.at[2, k]))

    def run(first_step):
        if first and first_step:
            x = x_refs[0][...]
        elif first:
            x = jnp.where(i < ctx_tiles, x_refs[0][...], x_refs[1][...])
        else:
            x = x_refs[0][...]
        m = mod_ref[0, 0]
        shift, scale, gate = m[mrow:mrow + 1], m[mrow + 1:mrow + 2], m[mrow + 2:mrow + 3]
        h = (_rms(x, g_ref[0]) * (1.0 + scale) + shift).astype(BF16)
        acc = jnp.zeros(x.shape, F32)
        for c0 in range(0, d_ff, FFN_CHUNK):
            if first_step:
                for cp in weight_copies(c0):
                    cp.wait()
            g = _dot(h, wgu_ref[:, c0:c0 + FFN_CHUNK].astype(BF16))
            u = _dot(h, wgu_ref[:, d_ff + c0:d_ff + c0 + FFN_CHUNK].astype(BF16))
            a = (_silu(g) * u).astype(BF16)
            acc = acc + _dot(a, wd_ref[c0:c0 + FFN_CHUNK, :].astype(BF16))
        out = x + 0.5 * gate * acc
        if not final:
            refs[0][...] = out
            return
        gf_ref, yc_ref, yl_ref = refs[:3]
        y = _rms(out, gf_ref[...])
        if first_step:
            yc_ref[...] = y
            return

        @pl.when(i < ctx_tiles)
        def _():
            yc_ref[...] = y

        @pl.when(i >= ctx_tiles)
        def _():
            yl_ref[...] = y

    @pl.when(i == 0)
    def _():
        for c0 in range(0, d_ff, FFN_CHUNK):
            for cp in weight_copies(c0):
                cp.start()
        run(True)

    @pl.when(i > 0)
    def _():
        run(False)


def _ffn_call(xs, mod_l, g, w_gu, w_d, g_final, *, layer, which, mrow, tm, n_ctx_rows, dec_seq):
    first = len(xs) == 2
    final = g_final is not None
    d = xs[0].shape[1]
    t = sum(a.shape[0] for a in xs)
    d_ff = w_d.shape[2]
    ctx_tiles = n_ctx_rows // tm

    def ctx_map(i):
        return (jnp.minimum(i, ctx_tiles - 1), 0)

    def lat_map(i):
        return (jnp.maximum(i - ctx_tiles, 0), 0)

    x_specs = ([pl.BlockSpec((tm, d), ctx_map), pl.BlockSpec((tm, d), lat_map)] if first
               else [pl.BlockSpec((tm, d), lambda i: (i, 0))])
    in_specs = x_specs + [
        _mod_spec(mod_l, layer, _mod_index(tm, n_ctx_rows, dec_seq)),
        _gain_spec(*g),
        pl.BlockSpec(memory_space=pl.ANY),
        pl.BlockSpec(memory_space=pl.ANY),
    ]
    scratch = [pltpu.VMEM(w_gu.shape[2:], F32), pltpu.VMEM(w_d.shape[2:], F32),
               pltpu.SemaphoreType.DMA((3, d_ff // FFN_CHUNK))]
    args = list(xs) + [mod_l, g[0], w_gu, w_d]
    if final:
        in_specs.append(_resident((1, d)))
        args.append(g_final.reshape(1, d))
        out_shape = (jax.ShapeDtypeStruct((n_ctx_rows, d), F32), jax.ShapeDtypeStruct((t - n_ctx_rows, d), F32))
        out_specs = (pl.BlockSpec((tm, d), ctx_map), pl.BlockSpec((tm, d), lat_map))
    else:
        out_shape = jax.ShapeDtypeStruct((t, d), F32)
        out_specs = pl.BlockSpec((tm, d), lambda i: (i, 0))
    return pl.pallas_call(
        functools.partial(_ffn_kernel, mrow=mrow, d_ff=d_ff, ctx_tiles=ctx_tiles, first=first, final=final,
                          layer=layer, which=which),
        out_shape=out_shape,
        grid=(t // tm,),
        in_specs=in_specs,
        out_specs=out_specs,
        scratch_shapes=scratch,
        compiler_params=pltpu.CompilerParams(
            dimension_semantics=("arbitrary",), vmem_limit_bytes=VMEM_LIMIT),
        name="ffn_final" if final else ("ffn_first" if first else "ffn"),
    )(*args)


A_WIDTH = A_GROUPS * LANES
SSM_INNER = SSM_HEADS * SSM_HEAD_DIM
SSM_BC = SSM_GROUPS * SSM_STATE
EV_UV = 0
EV_Z = EV_UV + 2 * A_WIDTH
EV_X = EV_Z + SSM_INNER
EV_B = EV_X + SSM_INNER
EV_C = EV_B + SSM_BC
EV_DT = EV_C + SSM_BC


def _gelu_tanh(x):
    c, a = 0.7978845608028654, 0.044715
    hx = 0.5 * x
    return hx + hx * jnp.tanh(x * (c + (c * a) * (x * x)))


def _softplus(x):
    return jnp.maximum(x, 0.0) + jnp.log1p(jnp.exp(-jnp.abs(x)))


def _split3(v):
    hi = v.astype(BF16)
    r = v - hi.astype(F32)
    mid = r.astype(BF16)
    lo = (r - mid.astype(F32)).astype(BF16)
    return hi, mid, lo


def _expand(parts, e):
    n = parts[0].shape[0]
    y = _dot(jnp.concatenate(parts, axis=0), e)
    out = y[0:n]
    for k in range(1, len(parts)):
        out = out + y[k * n:(k + 1) * n]
    return out


def _even_kernel(x_ref, mod_ref, g_ref, win_ref, wdt_ref, wout_ref, ws_ref, bs_ref, gv_ref, wc_ref, bc_ref,
                 dtb_ref, alog_ref, dsk_ref, go_ref, e_ref, *rest, seq_len, n_seq, has_h0, emit_state,
                 n_carry, out_slot):
    rest = list(rest)
    h0_ref = rest.pop(0) if has_h0 else None
    rest = rest[n_carry:]
    o_ref = rest.pop(0)
    st_ref = rest.pop(0) if emit_state else None
    h_ref, mix_ref, xs_ref, b_ref, c_ref, dt_ref, cum_ref, y_ref, state_ref = rest

    rows = seq_len * n_seq
    ck = SSM_CHUNK
    n_chunk = seq_len // ck
    m = mod_ref[0, 0]
    h_ref[...] = (_rms(x_ref[...], g_ref[0]) * (1.0 + m[4:5]) + m[3:4]).astype(BF16)

    vb = _rms(_gelu_tanh(_dot(h_ref[...], win_ref[:, EV_UV + A_WIDTH:EV_Z])), gv_ref[...]).astype(BF16)
    for g in range(A_GROUPS):
        l0 = g * LANES
        u = _gelu_tanh(_dot(h_ref[...], win_ref[:, EV_UV + l0:EV_UV + l0 + LANES]))
        for cc in range(rows // A_CHUNK):
            r0 = cc * A_CHUNK
            s = _dot(ws_ref[g], vb[r0:r0 + A_CHUNK, l0:l0 + LANES]) + bs_ref[g]
            mix_ref[r0:r0 + A_CHUNK, l0:l0 + LANES] = (u[r0:r0 + A_CHUNK] * s).astype(BF16)

    pw = 2 * LANES
    rin = lax.broadcasted_iota(jnp.int32, (rows, pw), 0) % seq_len
    for c0 in range(0, EV_DT - EV_X, pw):
        xbc = _dot(h_ref[...], win_ref[:, EV_X + c0:EV_X + c0 + pw])
        wc = wc_ref[:, c0:c0 + pw]
        prev = jnp.where(rin >= 1, pltpu.roll(xbc, 1, 0), 0.0)
        nxt = jnp.where(rin < seq_len - 1, pltpu.roll(xbc, rows - 1, 0), 0.0)
        xbc = _silu(prev * wc[0:1] + xbc * wc[1:2] + nxt * wc[2:3] + bc_ref[:, c0:c0 + pw])
        if c0 < SSM_INNER:
            xs_ref[:, c0:c0 + pw] = xbc
        elif c0 < SSM_INNER + SSM_BC:
            b_ref[:, c0 - SSM_INNER:c0 - SSM_INNER + pw] = xbc.astype(BF16)
        else:
            c_ref[:, c0 - SSM_INNER - SSM_BC:c0 - SSM_INNER - SSM_BC + pw] = xbc.astype(BF16)

    dt = _softplus(_dot(h_ref[...], wdt_ref[...]) + dtb_ref[...])
    lane = lax.broadcasted_iota(jnp.int32, dt.shape, 1)
    rck = lax.broadcasted_iota(jnp.int32, dt.shape, 0) % ck
    da = jnp.where(lane < 2 * SSM_HEADS, dt * -jnp.exp(alog_ref[...]), 0.0)
    cf = da
    cb = da
    sh = 1
    while sh < ck:
        cf = cf + jnp.where(rck >= sh, pltpu.roll(cf, sh, 0), 0.0)
        cb = cb + jnp.where(rck < ck - sh, pltpu.roll(cb, rows - sh, 0), 0.0)
        sh *= 2
    dt_ref[...] = dt
    cum_ref[...] = jnp.where(lane < SSM_HEADS, cf, cb)

    ii = lax.broadcasted_iota(jnp.int32, (ck, ck), 0)
    jj = lax.broadcasted_iota(jnp.int32, (ck, ck), 1)
    lane_lo = lax.broadcasted_iota(jnp.int32, (ck, LANES), 1) < SSM_HEAD_DIM
    gw = SSM_INNER // SSM_GROUPS
    hpg = SSM_HEADS // SSM_GROUPS

    def chunk_step(r0, direction, slot):
        e = e_ref[:, direction * SSM_INNER:(direction + 1) * SSM_INNER]
        cum_n = cum_ref[pl.ds(r0, ck), :]
        cum_t = cum_n.T
        dt_n = dt_ref[pl.ds(r0, ck), :]
        dt_e = jnp.concatenate(
            [jnp.where(lane_lo,
                       jnp.broadcast_to(dt_n[:, direction * SSM_HEADS + hd:direction * SSM_HEADS + hd + 1], (ck, LANES)),
                       jnp.broadcast_to(dt_n[:, direction * SSM_HEADS + hd + 1:direction * SSM_HEADS + hd + 2], (ck, LANES)))
             for hd in range(0, SSM_HEADS, 2)], axis=1)
        cum_e = _expand(_split3(cum_n), e)
        x_c = xs_ref[pl.ds(r0, ck), :]
        xdt = x_c * dt_e
        if direction == 0:
            tot = cum_e[ck - 1:ck, :]
            tri = ii >= jj
        else:
            tot = cum_e[0:1, :]
            tri = ii <= jj
        ec = jnp.exp(cum_e)
        xd = (xdt * jnp.exp(tot - cum_e)).astype(BF16)
        bm = b_ref[pl.ds(r0, ck), :]
        cm = c_ref[pl.ds(r0, ck), :]
        st = state_ref[slot]
        y_parts = []
        st_parts = []
        for g in range(SSM_GROUPS):
            bg = bm[:, g * SSM_STATE:(g + 1) * SSM_STATE]
            cg = cm[:, g * SSM_STATE:(g + 1) * SSM_STATE]
            cbm = _dot_nt(cg, bg)
            for pr in range(hpg // 2):
                hd = g * hpg + 2 * pr
                l0 = hd * SSM_HEAD_DIM
                sc = []
                for k in range(2):
                    col = direction * SSM_HEADS + hd + k
                    seg = cum_n[:, col:col + 1] - cum_t[col:col + 1, :]
                    sc.append((cbm * jnp.where(tri, jnp.exp(seg), 0.0)).astype(BF16))
                xp = xdt[:, l0:l0 + LANES]
                rhs = jnp.concatenate([jnp.where(lane_lo, xp, 0.0), jnp.where(lane_lo, 0.0, xp)],
                                      axis=0).astype(BF16)
                y_parts.append(_dot(jnp.concatenate(sc, axis=1), rhs))
            y_off = _dot(cg, st[:, g * gw:(g + 1) * gw].astype(BF16))
            y_parts[-2] = y_parts[-2] + y_off[:, 0:LANES] * ec[:, g * gw:g * gw + LANES]
            y_parts[-1] = y_parts[-1] + y_off[:, LANES:gw] * ec[:, g * gw + LANES:(g + 1) * gw]
            st_parts.append(_dot_tn(bg, xd[:, g * gw:(g + 1) * gw]))
        y_ref[direction, pl.ds(r0, ck), :] = jnp.concatenate(y_parts, axis=1)
        state_ref[slot] = st * jnp.exp(tot) + jnp.concatenate(st_parts, axis=1)

    for s in range(n_seq):
        for direction in range(2):
            if has_h0:
                state_ref[2 * s + direction] = h0_ref[s, 0, direction].T
            else:
                state_ref[2 * s + direction] = jnp.zeros(state_ref.shape[1:], F32)

    def scan_body(k, carry):
        for s in range(n_seq):
            for direction in range(2):
                c = k if direction == 0 else n_chunk - 1 - k
                chunk_step(pl.multiple_of(s * seq_len + c * ck, ck), direction, 2 * s + direction)
        return carry

    lax.fori_loop(0, n_chunk, scan_body, 0)
    if emit_state:
        for s in range(n_seq):
            for direction in range(2):
                st_ref[s, out_slot, direction] = state_ref[2 * s + direction].T
        for slot in range(st_ref.shape[1]):
            if slot != out_slot:
                st_ref[:, slot] = jnp.zeros((n_seq,) + st_ref.shape[2:], F32)

    z = _dot(h_ref[...], win_ref[:, EV_Z:EV_X])
    y = y_ref[0] + y_ref[1] + xs_ref[...] * (dsk_ref[0:1] + dsk_ref[1:2])
    mix_ref[:, A_WIDTH:] = _rms(y * _silu(z), go_ref[...]).astype(BF16)
    for c0 in range(0, o_ref.shape[1], pw):
        o_ref[:, c0:c0 + pw] = (x_ref[:, c0:c0 + pw]
                                + m[5:6, c0:c0 + pw] * _dot(mix_ref[...], wout_ref[:, c0:c0 + pw]))


def _even_call(x, mod_l, g, p, h0, *, layer, layer_idx, seq_len, n_seq, row0, n_tiles, mod_set, batch,
               n_layers=1, carry=None):
    t, d = x.shape
    rows = seq_len * n_seq
    tile0 = row0 // rows
    has_h0 = h0 is not None
    emit_state = not has_h0
    hp, ns = SSM_INNER, SSM_STATE
    in_specs = [
        pl.BlockSpec((rows, d), lambda i: (tile0 + i, 0)),
        _mod_spec(mod_l, layer, mod_set),
        _gain_spec(*g),
        _resident(p["w_in"].shape),
        _resident(p["w_dt"].shape),
        _resident(p["w_out"].shape),
        _resident(p["w_s"].shape),
        _resident(p["b_s"].shape),
        _resident(p["g_v"].shape),
        _resident(p["w_conv"].shape),
        _resident(p["b_conv"].shape),
        _resident(p["dt_bias"].shape),
        _resident(p["a_log"].shape),
        _resident(p["d_skip"].shape),
        _resident(p["g_out"].shape),
        _resident(p["expand"].shape),
    ]
    args = [x, mod_l, g[0], p["w_in"], p["w_dt"], p["w_out"], p["w_s"], p["b_s"], p["g_v"], p["w_conv"],
            p["b_conv"], p["dt_bias"], p["a_log"], p["d_skip"], p["g_out"], p["expand"]]
    out_shape = [jax.ShapeDtypeStruct((t, d), F32)]
    out_specs = [pl.BlockSpec((rows, d), lambda i: (tile0 + i, 0))]
    if has_h0:
        in_specs.append(pl.BlockSpec((n_seq, 1, 2, hp, ns), lambda i: (i, layer_idx, 0, 0, 0)))
        args.append(h0)
    aliases = {0: 0}
    if emit_state:
        out_shape.append(jax.ShapeDtypeStruct((batch, n_layers, 2, hp, ns), F32))
        if carry is None:
            out_specs.append(pl.BlockSpec((n_seq, n_layers, 2, hp, ns), lambda i: (i, 0, 0, 0, 0)))
        else:
            out_specs.append(pl.BlockSpec((n_seq, 1, 2, hp, ns), lambda i: (i, layer_idx, 0, 0, 0)))
            in_specs.append(pl.BlockSpec(memory_space=pl.ANY))
            args.append(carry)
            aliases[len(args) - 1] = 1
    out_slot = layer_idx if carry is None else 0
    scratch = [
        pltpu.VMEM((rows, d), BF16),
        pltpu.VMEM((rows, A_WIDTH + SSM_INNER), BF16),
        pltpu.VMEM((rows, SSM_INNER), F32),
        pltpu.VMEM((rows, SSM_BC), BF16),
        pltpu.VMEM((rows, SSM_BC), BF16),
        pltpu.VMEM((rows, LANES), F32),
        pltpu.VMEM((rows, LANES), F32),
        pltpu.VMEM((2, rows, SSM_INNER), F32),
        pltpu.VMEM((2 * n_seq, SSM_STATE, SSM_INNER), F32),
    ]
    res = pl.pallas_call(
        functools.partial(_even_kernel, seq_len=seq_len, n_seq=n_seq, has_h0=has_h0, emit_state=emit_state,
                          n_carry=len(aliases) - 1, out_slot=out_slot),
        out_shape=out_shape,
        grid=(n_tiles,),
        in_specs=in_specs,
        out_specs=out_specs,
        scratch_shapes=scratch,
        input_output_aliases=aliases,
        compiler_params=pltpu.CompilerParams(
            dimension_semantics=("arbitrary",), vmem_limit_bytes=VMEM_LIMIT),
        name="even_mixer_latent" if has_h0 else "even_mixer_context",
    )(*args)
    return res


def _pack_even(w_in, w_out, w_s, b_s, g_v, w_conv, b_conv, dt_bias, a_log, d_skip, g_out):
    d = w_in.shape[0]
    n_dt = 2 * SSM_HEADS
    w_dt = jnp.concatenate([w_in[:, EV_DT:], jnp.zeros((d, LANES - n_dt), F32)], axis=1)

    def narrow(v):
        return jnp.concatenate([v.reshape(1, n_dt), jnp.zeros((1, LANES - n_dt), F32)], axis=1)

    lane_head = jnp.arange(2 * SSM_INNER) // SSM_HEAD_DIM
    expand = (jnp.arange(LANES)[:, None] == lane_head[None, :]).astype(BF16)
    return {
        "w_in": w_in[:, :EV_DT].astype(BF16),
        "w_dt": w_dt.astype(BF16),
        "w_out": w_out.astype(BF16),
        "w_s": w_s.astype(BF16),
        "b_s": jnp.broadcast_to(b_s[:, :, None], b_s.shape + (LANES,)),
        "g_v": g_v.reshape(1, -1),
        "w_conv": w_conv,
        "b_conv": b_conv.reshape(1, -1),
        "dt_bias": narrow(dt_bias),
        "a_log": narrow(a_log),
        "d_skip": jnp.repeat(d_skip, SSM_HEAD_DIM, axis=1),
        "g_out": g_out.reshape(1, -1),
        "expand": expand,
    }


MLA_Q_RANK = 384
MLA_KV_RANK = 256
CONV_WIDTH = 512
HEAD_W = LANES
ROPE_L0 = MLA_NOPE
OD_CQ = 0
OD_CKV = OD_CQ + MLA_Q_RANK
OD_LAT = OD_CKV + MLA_KV_RANK
CONV_PAD = 16
CONV_RB = 64
ATT_QB = 256


def _odd_kernel(x_ref, mod_ref, g_ref, win_ref, wag_ref, wkr_ref, wuq_ref, wuk_ref, wuv_ref, wout_ref, gcq_ref, gckv_ref,
                wdw_ref, bdw_ref, gln_ref, bln_ref, *rest, seq_len, n_seq, past, n_carry, out_slot):
    rest = list(rest)
    has_cache = past > 0
    if has_cache:
        cckv_ref, ckr_ref, cos_ref, sin_ref = rest[:4]
        o_ref = rest[4]
        rest = rest[5:]
    else:
        rest = rest[n_carry:]
        o_ref, ckv_out_ref, kr_out_ref = rest[:3]
        rest = rest[3:]
    h_ref, cq_ref, ckv_ref, kr_ref, mix_ref, q_ref, k_ref, v_ref, oh_ref, pad_ref, cout_ref = rest

    nh = MLA_HEADS
    hw = nh * MLA_V
    att_qb = min(ATT_QB, seq_len)
    first_head_lanes = lax.broadcasted_iota(jnp.int32, (att_qb, HEAD_W), 1) < MLA_V
    n_keys = past + seq_len
    scale = (MLA_NOPE + MLA_ROPE) ** -0.5
    m = mod_ref[0, 0]
    h_ref[...] = (_rms(x_ref[...], g_ref[0]) * (1.0 + m[4:5]) + m[3:4]).astype(BF16)

    cq_ref[...] = _rms(_dot(h_ref[...], win_ref[:, OD_CQ:OD_CKV]), gcq_ref[...]).astype(BF16)
    ckv = _rms(_dot(h_ref[...], win_ref[:, OD_CKV:OD_LAT]), gckv_ref[...])
    ckv_ref[...] = ckv.astype(BF16)
    kr = _dot(h_ref[...], wkr_ref[:, 0:HEAD_W])
    if has_cache:
        cos = jnp.concatenate([cos_ref[...]] * n_seq, axis=0)
        sin = jnp.concatenate([sin_ref[...]] * n_seq, axis=0)
        kr = kr * cos + _dot(h_ref[...], wkr_ref[:, HEAD_W:2 * HEAD_W]) * sin
    else:
        for s in range(n_seq):
            ckv_out_ref[s, out_slot] = ckv[s * seq_len:(s + 1) * seq_len]
            kr_out_ref[s, out_slot] = kr[s * seq_len:(s + 1) * seq_len, ROPE_L0:ROPE_L0 + MLA_ROPE]
        for slot in range(ckv_out_ref.shape[1]):
            if slot != out_slot:
                ckv_out_ref[:, slot] = jnp.zeros((n_seq,) + ckv_out_ref.shape[2:], F32)
                kr_out_ref[:, slot] = jnp.zeros((n_seq,) + kr_out_ref.shape[2:], F32)
    kr_ref[...] = kr

    n_pair = nh // 2

    def twice(a):
        return jnp.concatenate([a, a], axis=1)

    def q_body(pr, carry):
        q = _dot(cq_ref[...], wuq_ref[pr])
        if has_cache:
            q = q * twice(cos) + _dot(cq_ref[...], wuq_ref[n_pair + pr]) * twice(sin)
        q_ref[pr] = q.astype(BF16)
        return carry

    lax.fori_loop(0, n_pair, q_body, 0, unroll=2)

    n_qb = seq_len // att_qb
    for s in range(n_seq):
        r0 = s * seq_len

        def kv_body(pr, carry, s=s, r0=r0):
            lat = ckv_ref[r0:r0 + seq_len]
            k_ref[pr, past:n_keys] = (_dot(lat, wuk_ref[pr]) + twice(kr_ref[r0:r0 + seq_len])).astype(BF16)
            v_ref[pr, past:n_keys] = _dot(lat, wuv_ref[pr]).astype(BF16)
            if has_cache:
                lat_c = cckv_ref[s, 0].astype(BF16)
                k_ref[pr, 0:past] = (_dot(lat_c, wuk_ref[pr]) + twice(ckr_ref[s, 0])).astype(BF16)
                v_ref[pr, 0:past] = _dot(lat_c, wuv_ref[pr]).astype(BF16)
            return carry

        lax.fori_loop(0, n_pair, kv_body, 0, unroll=2)

        def att_body(it, carry, r0=r0):
            pr = it // n_qb
            q0 = pl.multiple_of(r0 + (it % n_qb) * att_qb, att_qb)
            outs = []
            for l0 in range(0, 2 * HEAD_W, HEAD_W):
                sc = _dot_nt(q_ref[pr, pl.ds(q0, att_qb), l0:l0 + HEAD_W], k_ref[pr, :, l0:l0 + HEAD_W]) * scale
                p = jnp.exp(sc - jnp.max(sc, axis=-1, keepdims=True))
                den = jnp.sum(p, axis=-1, keepdims=True)
                outs.append(_dot(p.astype(BF16), v_ref[pr]) / den)
            oh_ref[pr, pl.ds(q0, att_qb), :] = jnp.where(first_head_lanes, outs[0], outs[1]).astype(BF16)
            return carry

        lax.fori_loop(0, n_pair * n_qb, att_body, 0, unroll=4)

    for pr in range(n_pair):
        mix_ref[:, pr * HEAD_W:(pr + 1) * HEAD_W] = oh_ref[pr]

    pw = 2 * LANES
    n_lt = CONV_WIDTH // LANES
    nv = CONV_RB // 8
    zpad = jnp.zeros((n_lt, CONV_PAD, LANES), F32)
    for s in range(n_seq):
        r0 = s * seq_len
        pad_ref[:, 0:CONV_PAD] = zpad
        pad_ref[:, CONV_PAD + seq_len:2 * CONV_PAD + seq_len] = zpad
        for c0 in range(0, CONV_WIDTH, pw):
            a = _dot(h_ref[r0:r0 + seq_len], wag_ref[:, c0:c0 + pw])
            gt = _dot(h_ref[r0:r0 + seq_len], wag_ref[:, CONV_WIDTH + c0:CONV_WIDTH + c0 + pw])
            glu = a * _sigmoid(gt)
            for l0 in range(0, pw, LANES):
                pad_ref[(c0 + l0) // LANES, CONV_PAD:CONV_PAD + seq_len] = glu[:, l0:l0 + LANES]

        def conv_body(b, carry, r0=r0):
            base = pl.multiple_of(b * CONV_RB, CONV_RB)
            for lt in range(n_lt):
                l0 = lt * LANES
                accs = [jnp.broadcast_to(bdw_ref[:, l0:l0 + LANES], (8, LANES))] * nv
                for k in range(CONV_K):
                    wk = jnp.broadcast_to(wdw_ref[k:k + 1, l0:l0 + LANES], (8, LANES))
                    off = k + CONV_PAD - CONV_K // 2
                    accs = [accs[v] + wk * pad_ref[lt, pl.ds(base + v + off, 8, stride=nv), :] for v in range(nv)]
                for v in range(nv):
                    cout_ref[lt, pl.ds(base + v, 8, stride=nv), :] = accs[v]
            acc = jnp.concatenate([cout_ref[lt, pl.ds(base, CONV_RB), :] for lt in range(n_lt)], axis=1)
            dc = acc - jnp.mean(acc, axis=-1, keepdims=True)
            var = jnp.mean(dc * dc, axis=-1, keepdims=True)
            dn = dc * lax.rsqrt(var + EPS) * gln_ref[...] + bln_ref[...]
            mix_ref[pl.ds(r0 + base, CONV_RB), hw:] = _silu(dn).astype(BF16)
            return carry

        lax.fori_loop(0, seq_len // CONV_RB, conv_body, 0, unroll=2)

    for c0 in range(0, o_ref.shape[1], pw):
        o_ref[:, c0:c0 + pw] = (x_ref[:, c0:c0 + pw]
                                + m[5:6, c0:c0 + pw] * _dot(mix_ref[...], wout_ref[:, c0:c0 + pw]))


def _odd_call(x, mod_l, g, p, cache, *, layer, layer_idx, seq_len, n_seq, row0, n_tiles, mod_set, batch,
              n_layers=1, carry=None):
    t, d = x.shape
    rows = seq_len * n_seq
    tile0 = row0 // rows
    has_cache = cache is not None
    past = cache[0].shape[2] if has_cache else 0
    names = ["w_in", "w_ag", "w_kr", "w_uq", "w_uk", "w_uv", "w_out", "g_cq", "g_ckv", "w_dw", "b_dw", "g_ln", "b_ln"]
    in_specs = [
        pl.BlockSpec((rows, d), lambda i: (tile0 + i, 0)),
        _mod_spec(mod_l, layer, mod_set),
        _gain_spec(*g),
    ] + [_resident(p[k].shape) for k in names]
    args = [x, mod_l, g[0]] + [p[k] for k in names]
    out_shape = [jax.ShapeDtypeStruct((t, d), F32)]
    out_specs = [pl.BlockSpec((rows, d), lambda i: (tile0 + i, 0))]
    aliases = {0: 0}
    if has_cache:
        cckv, ckr, cos, sin = cache
        in_specs += [
            pl.BlockSpec((n_seq, 1, past, MLA_KV_RANK), lambda i: (i, layer_idx, 0, 0)),
            pl.BlockSpec((n_seq, 1, past, HEAD_W), lambda i: (i, layer_idx, 0, 0)),
            _resident(cos.shape),
            _resident(sin.shape),
        ]
        args += [cckv, ckr, cos, sin]
    else:
        out_shape += [jax.ShapeDtypeStruct((batch, n_layers, seq_len, MLA_KV_RANK), F32),
                      jax.ShapeDtypeStruct((batch, n_layers, seq_len, MLA_ROPE), F32)]
        if carry is None:
            out_specs += [pl.BlockSpec((n_seq, n_layers, seq_len, MLA_KV_RANK), lambda i: (i, 0, 0, 0)),
                          pl.BlockSpec((n_seq, n_layers, seq_len, MLA_ROPE), lambda i: (i, 0, 0, 0))]
        else:
            out_specs += [pl.BlockSpec((n_seq, 1, seq_len, MLA_KV_RANK), lambda i: (i, layer_idx, 0, 0)),
                          pl.BlockSpec((n_seq, 1, seq_len, MLA_ROPE), lambda i: (i, layer_idx, 0, 0))]
            for k, arr in enumerate(carry):
                in_specs.append(pl.BlockSpec(memory_space=pl.ANY))
                args.append(arr)
                aliases[len(args) - 1] = 1 + k
    n_keys = past + seq_len
    scratch = [
        pltpu.VMEM((rows, d), BF16),
        pltpu.VMEM((rows, MLA_Q_RANK), BF16),
        pltpu.VMEM((rows, MLA_KV_RANK), BF16),
        pltpu.VMEM((rows, HEAD_W), F32),
        pltpu.VMEM((rows, MLA_HEADS * MLA_V + CONV_WIDTH), BF16),
        pltpu.VMEM((MLA_HEADS // 2, rows, 2 * HEAD_W), BF16),
        pltpu.VMEM((MLA_HEADS // 2, n_keys, 2 * HEAD_W), BF16),
        pltpu.VMEM((MLA_HEADS // 2, n_keys, 2 * MLA_V), BF16),
        pltpu.VMEM((MLA_HEADS // 2, rows, 2 * MLA_V), BF16),
        pltpu.VMEM((CONV_WIDTH // LANES, seq_len + 2 * CONV_PAD, LANES), F32),
        pltpu.VMEM((CONV_WIDTH // LANES, seq_len, LANES), F32),
    ]
    return pl.pallas_call(
        functools.partial(_odd_kernel, seq_len=seq_len, n_seq=n_seq, past=past, n_carry=len(aliases) - 1,
                          out_slot=layer_idx if (not has_cache and carry is None) else 0),
        out_shape=out_shape,
        grid=(n_tiles,),
        in_specs=in_specs,
        out_specs=out_specs,
        scratch_shapes=scratch,
        input_output_aliases=aliases,
        compiler_params=pltpu.CompilerParams(
            dimension_semantics=("arbitrary",), vmem_limit_bytes=VMEM_LIMIT),
        name="odd_mixer_latent" if has_cache else "odd_mixer_context",
    )(*args)


def _head_tiles(w, per_head, start, width, lane0):
    k = w.shape[0]
    blk = w.reshape(k, MLA_HEADS, per_head)[:, :, start:start + width]
    out = jnp.zeros((k, MLA_HEADS, HEAD_W), w.dtype).at[:, :, lane0:lane0 + width].set(blk)
    return out.reshape(k, MLA_HEADS // 2, 2 * HEAD_W).transpose(1, 0, 2)


def _rope_swap(w):
    q = MLA_ROPE // 4
    idx = jnp.concatenate([jnp.arange(q, 2 * q), jnp.arange(0, q), jnp.arange(3 * q, 4 * q), jnp.arange(2 * q, 3 * q)])
    return w[..., idx]


def _pack_odd(w_in, w_out, g_cq, w_uq, g_ckv, w_ukv, w_dw, b_dw, g_ln, b_ln):
    d = w_in.shape[0]
    o1 = MLA_Q_RANK
    o2 = o1 + MLA_KV_RANK
    o3 = o2 + MLA_ROPE
    w_kr = w_in[:, o2:o3]

    def kr_tile(w):
        return jnp.zeros((d, HEAD_W), F32).at[:, ROPE_L0:ROPE_L0 + MLA_ROPE].set(w)

    w_kr_p = jnp.concatenate([kr_tile(w_kr), kr_tile(_rope_swap(w_kr))], axis=1)
    qh = MLA_NOPE + MLA_ROPE
    rank = w_uq.shape[0]
    w_q3 = w_uq.reshape(rank, MLA_HEADS, qh)
    w_q_sw = jnp.concatenate([jnp.zeros((rank, MLA_HEADS, MLA_NOPE), F32), _rope_swap(w_q3[:, :, MLA_NOPE:])],
                             axis=2).reshape(rank, MLA_HEADS * qh)
    w_uq_p = jnp.concatenate([_head_tiles(w_uq, qh, 0, qh, 0),
                              _head_tiles(w_q_sw, qh, MLA_NOPE, MLA_ROPE, ROPE_L0)], axis=0)
    kvh = MLA_NOPE + MLA_V
    w_uk_p = _head_tiles(w_ukv, kvh, 0, MLA_NOPE, 0)
    rank_kv = w_ukv.shape[0]
    w_uv_p = (w_ukv.reshape(rank_kv, MLA_HEADS, kvh)[:, :, MLA_NOPE:]
              .reshape(rank_kv, MLA_HEADS // 2, 2 * MLA_V).transpose(1, 0, 2))
    return {
        "w_in": w_in[:, :o2].astype(BF16),
        "w_ag": w_in[:, o3:].astype(BF16),
        "w_kr": w_kr_p.astype(BF16),
        "w_uq": w_uq_p.astype(BF16),
        "w_uk": w_uk_p.astype(BF16),
        "w_uv": w_uv_p.astype(BF16),
        "w_out": w_out.astype(BF16),
        "g_cq": g_cq.reshape(1, -1),
        "g_ckv": g_ckv.reshape(1, -1),
        "w_dw": w_dw,
        "b_dw": b_dw.reshape(1, -1),
        "g_ln": g_ln.reshape(1, -1),
        "b_ln": b_ln.reshape(1, -1),
    }


def _rope_tables(length):
    n_freq = MLA_ROPE // 4
    pos = jnp.arange(length)
    row = (pos // GRID_W).astype(F32)
    col = (pos % GRID_W).astype(F32)
    freqs = ROPE_THETA ** (-jnp.arange(n_freq, dtype=F32) / n_freq)
    ar = row[:, None] * freqs
    ac = col[:, None] * freqs
    cos = jnp.concatenate([jnp.cos(ar), jnp.cos(ar), jnp.cos(ac), jnp.cos(ac)], axis=1)
    sin = jnp.concatenate([-jnp.sin(ar), jnp.sin(ar), -jnp.sin(ac), jnp.sin(ac)], axis=1)
    tail = HEAD_W - ROPE_L0 - MLA_ROPE
    cos_t = jnp.concatenate([jnp.ones((length, ROPE_L0), F32), cos, jnp.zeros((length, tail), F32)], axis=1)
    sin_t = jnp.concatenate([jnp.zeros((length, ROPE_L0), F32), sin, jnp.zeros((length, tail), F32)], axis=1)
    return cos_t, sin_t


def kernel(x_prompt, x_sample, state_ssd, cache_mla_ckv, cache_mla_krope, c, c_ctx, w_mod, b_mod, g_norm, w_ff_gu, w_ff_down, w_in_even, w_out_even, w_spatial, b_spatial, g_gmlp_v, w_conv_ssm, b_conv_ssm, dt_bias, a_log, d_skip, g_ssm_out, w_in_odd, w_out_odd, g_cq, w_uq, g_ckv, w_ukv, w_dwconv, b_dwconv, g_conv_ln, b_conv_ln, g_final):
    bp, seq, d = x_prompt.shape
    db, dec_seq, _ = x_sample.shape
    n_ctx = bp * seq
    cond_t = jnp.concatenate([c_ctx[None], c], axis=0).T
    mod = _mod_call(cond_t, w_mod, b_mod).reshape(w_mod.shape[0], 1 + db, N_MOD, d)
    depth = w_mod.shape[0]
    ffn = functools.partial(_ffn_call, tm=512, n_ctx_rows=n_ctx, dec_seq=dec_seq)
    ctx_seqs = 4
    ctx_tile = dict(seq_len=seq, n_seq=ctx_seqs, row0=0, n_tiles=bp // ctx_seqs, mod_set=lambda i: 0, batch=bp)
    lat_tile = dict(seq_len=dec_seq, n_seq=1, row0=n_ctx, n_tiles=db, mod_set=lambda i: 1 + i, batch=bp)
    h0 = state_ssd.reshape(db, state_ssd.shape[1], 2, SSM_INNER, SSM_STATE)
    ckr_tile = jnp.pad(cache_mla_krope, ((0, 0), (0, 0), (0, 0), (ROPE_L0, HEAD_W - ROPE_L0 - MLA_ROPE)))
    cos_t, sin_t = _rope_tables(dec_seq)
    n_even, n_odd = (depth + 1) // 2, depth // 2
    new_state = new_cache = None
    xs = [x_prompt.reshape(n_ctx, d), x_sample.reshape(db * dec_seq, d)]
    n_gain = g_norm.shape[1]
    gains = g_norm.reshape(depth * n_gain, 1, d)
    for l in range(depth):
        i = l // 2
        last = l == depth - 1
        x = ffn(xs, mod, (gains, l * n_gain), w_ff_gu, w_ff_down, None, layer=l, which=0, mrow=0)
        g_mix = (gains, l * n_gain + 1)
        if l % 2 == 0:
            pe = _pack_even(w_in_even[i], w_out_even[i], w_spatial[i], b_spatial[i], g_gmlp_v[i], w_conv_ssm[i],
                            b_conv_ssm[i], dt_bias[i], a_log[i], d_skip[i], g_ssm_out[i])
            x, new_state = _even_call(x, mod, g_mix, pe, None, layer=l, layer_idx=i, n_layers=n_even,
                                      carry=new_state, **ctx_tile)
            (x,) = _even_call(x, mod, g_mix, pe, h0, layer=l, layer_idx=i, **lat_tile)
        else:
            po = _pack_odd(w_in_odd[i], w_out_odd[i], g_cq[i], w_uq[i], g_ckv[i], w_ukv[i], w_dwconv[i],
                           b_dwconv[i], g_conv_ln[i], b_conv_ln[i])
            x, *new_cache = _odd_call(x, mod, g_mix, po, None, layer=l, layer_idx=i, n_layers=n_odd,
                                      carry=new_cache, **ctx_tile)
            (x,) = _odd_call(x, mod, g_mix, po, (cache_mla_ckv, ckr_tile, cos_t, sin_t),
                             layer=l, layer_idx=i, **lat_tile)
        res = ffn([x], mod, (gains, l * n_gain + 2), w_ff_gu, w_ff_down, g_final if last else None,
                  layer=l, which=1, mrow=6)
        xs = [res]
    y_ctx, y_lat = res
    new_state = new_state.reshape(bp, n_even, 2, SSM_HEADS, SSM_HEAD_DIM, SSM_STATE)
    return (y_ctx.reshape(bp, seq, d), y_lat.reshape(db, dec_seq, d), new_state, new_cache[0], new_cache[1])
```

```python
import functools

import jax
import jax.numpy as jnp
from jax import lax
from jax.experimental import pallas as pl
from jax.experimental.pallas import tpu as pltpu

F32 = jnp.float32
BF16 = jnp.bfloat16

EPS = 1e-6
N_MOD = 9
LANES = 128
A_GROUPS = 4
A_CHUNK = 128
SSM_HEADS = 8
SSM_HEAD_DIM = 64
SSM_GROUPS = 2
SSM_STATE = 128
SSM_CHUNK = 128
MLA_HEADS = 8
MLA_NOPE = 64
MLA_ROPE = 32
MLA_V = 64
CONV_K = 31
GRID_W = 64
ROPE_THETA = 10000.0

VMEM_LIMIT = 60 * 1024 * 1024


def _dot(a, b):
    return jnp.dot(a, b, preferred_element_type=F32)


def _dot_nt(a, b):
    return lax.dot_general(a, b, (((1,), (1,)), ((), ())), preferred_element_type=F32)


def _dot_tn(a, b):
    return lax.dot_general(a, b, (((0,), (0,)), ((), ())), preferred_element_type=F32)


def _rms(x, g):
    return x * lax.rsqrt(jnp.mean(x * x, axis=-1, keepdims=True) + EPS) * g


def _sigmoid(x):
    return 1.0 / (1.0 + jnp.exp(-x))


def _silu(x):
    return x * _sigmoid(x)


def _mod_index(tile_rows, n_ctx_rows, dec_seq):
    def index(i):
        return jnp.maximum((i * tile_rows - n_ctx_rows) // dec_seq + 1, 0)
    return index


def _mod_spec(mod_all, layer, set_of_tile):
    return pl.BlockSpec((1, 1) + mod_all.shape[2:], lambda i: (layer, set_of_tile(i), 0, 0))


def _gain_spec(g_all, idx):
    return pl.BlockSpec((1,) + g_all.shape[1:], lambda i: (idx, 0, 0), pipeline_mode=pl.Buffered(1))


def _resident(shape):
    nd = len(shape)
    return pl.BlockSpec(shape, lambda i: (0,) * nd, pipeline_mode=pl.Buffered(1))


def _mod_kernel(ct_ref, w_ref, b_ref, o_ref, *, n_sets):
    w = w_ref[0]
    s = _silu(ct_ref[...])
    rows = [jnp.sum(w * s[:, r:r + 1], axis=0, keepdims=True) for r in range(n_sets)]
    o_ref[0] = jnp.concatenate(rows, axis=0) + b_ref[0]


def _mod_call(cond_t, w_mod, b_mod):
    depth, d, n = w_mod.shape
    n_sets = cond_t.shape[1]
    tn = n // 4
    return pl.pallas_call(
        functools.partial(_mod_kernel, n_sets=n_sets),
        out_shape=jax.ShapeDtypeStruct((depth, n_sets, n), F32),
        grid=(depth, n // tn),
        in_specs=[
            pl.BlockSpec((d, n_sets), lambda l, j: (0, 0)),
            pl.BlockSpec((1, d, tn), lambda l, j: (l, 0, j)),
            pl.BlockSpec((1, 1, tn), lambda l, j: (l, 0, j)),
        ],
        out_specs=pl.BlockSpec((1, n_sets, tn), lambda l, j: (l, 0, j)),
        compiler_params=pltpu.CompilerParams(
            dimension_semantics=("arbitrary", "arbitrary"), vmem_limit_bytes=VMEM_LIMIT),
        name="adaln_mod",
    )(cond_t, w_mod, b_mod.reshape(depth, 1, n))


FFN_CHUNK = 2 * LANES


def _ffn_kernel(*refs, mrow, d_ff, ctx_tiles, first, final, layer, which):
    refs = list(refs)
    n_x = 2 if first else 1
    x_refs, (mod_ref, g_ref, wgu_hbm, wd_hbm), refs = refs[:n_x], refs[n_x:n_x + 4], refs[n_x + 4:]
    wgu_ref, wd_ref, sem = refs[-3:]
    i = pl.program_id(0)

    def weight_copies(c0):
        cols, rows = pl.ds(c0, FFN_CHUNK), pl.ds(c0, FFN_CHUNK)
        ucols = pl.ds(d_ff + c0, FFN_CHUNK)
        k = c0 // FFN_CHUNK
        return (pltpu.make_async_copy(wgu_hbm.at[layer, which, :, cols], wgu_ref.at[:, cols], sem.at[0, k]),
                pltpu.make_async_copy(wgu_hbm.at[layer, which, :, ucols], wgu_ref.at[:, ucols], sem.at[1, k]),
                pltpu.make_async_copy(wd_hbm.at[layer, which, rows, :], wd_ref.at[rows, :], sem.at[2, k]))

    def run(first_step):
        if first and first_step:
            x = x_refs[0][...]
        elif first:
            x = jnp.where(i < ctx_tiles, x_refs[0][...], x_refs[1][...])
        else:
            x = x_refs[0][...]
        m = mod_ref[0, 0]
        shift, scale, gate = m[mrow:mrow + 1], m[mrow + 1:mrow + 2], m[mrow + 2:mrow + 3]
        h = (_rms(x, g_ref[0]) * (1.0 + scale) + shift).astype(BF16)
        acc = jnp.zeros(x.shape, F32)
        for c0 in range(0, d_ff, FFN_CHUNK):
            if first_step:
                for cp in weight_copies(c0):
                    cp.wait()
            g = _dot(h, wgu_ref[:, c0:c0 + FFN_CHUNK].astype(BF16))
            u = _dot(h, wgu_ref[:, d_ff + c0:d_ff + c0 + FFN_CHUNK].astype(BF16))
            a = (_silu(g) * u).astype(BF16)
            acc = acc + _dot(a, wd_ref[c0:c0 + FFN_CHUNK, :].astype(BF16))
        out = x + 0.5 * gate * acc
        if not final:
            refs[0][...] = out
            return
        gf_ref, yc_ref, yl_ref = refs[:3]
        y = _rms(out, gf_ref[...])
        if first_step:
            yc_ref[...] = y
            return

        @pl.when(i < ctx_tiles)
        def _():
            yc_ref[...] = y

        @pl.when(i >= ctx_tiles)
        def _():
            yl_ref[...] = y

    @pl.when(i == 0)
    def _():
        for c0 in range(0, d_ff, FFN_CHUNK):
            for cp in weight_copies(c0):
                cp.start()
        run(True)

    @pl.when(i > 0)
    def _():
        run(False)


def _ffn_call(xs, mod_l, g, w_gu, w_d, g_final, *, layer, which, mrow, tm, n_ctx_rows, dec_seq):
    first = len(xs) == 2
    final = g_final is not None
    d = xs[0].shape[1]
    t = sum(a.shape[0] for a in xs)
    d_ff = w_d.shape[2]
    ctx_tiles = n_ctx_rows // tm

    def ctx_map(i):
        return (jnp.minimum(i, ctx_tiles - 1), 0)

    def lat_map(i):
        return (jnp.maximum(i - ctx_tiles, 0), 0)

    x_specs = ([pl.BlockSpec((tm, d), ctx_map), pl.BlockSpec((tm, d), lat_map)] if first
               else [pl.BlockSpec((tm, d), lambda i: (i, 0))])
    in_specs = x_specs + [
        _mod_spec(mod_l, layer, _mod_index(tm, n_ctx_rows, dec_seq)),
        _gain_spec(*g),
        pl.BlockSpec(memory_space=pl.ANY),
        pl.BlockSpec(memory_space=pl.ANY),
    ]
    scratch = [pltpu.VMEM(w_gu.shape[2:], F32), pltpu.VMEM(w_d.shape[2:], F32),
               pltpu.SemaphoreType.DMA((3, d_ff // FFN_CHUNK))]
    args = list(xs) + [mod_l, g[0], w_gu, w_d]
    if final:
        in_specs.append(_resident((1, d)))
        args.append(g_final.reshape(1, d))
        out_shape = (jax.ShapeDtypeStruct((n_ctx_rows, d), F32), jax.ShapeDtypeStruct((t - n_ctx_rows, d), F32))
        out_specs = (pl.BlockSpec((tm, d), ctx_map), pl.BlockSpec((tm, d), lat_map))
    else:
        out_shape = jax.ShapeDtypeStruct((t, d), F32)
        out_specs = pl.BlockSpec((tm, d), lambda i: (i, 0))
    return pl.pallas_call(
        functools.partial(_ffn_kernel, mrow=mrow, d_ff=d_ff, ctx_tiles=ctx_tiles, first=first, final=final,
                          layer=layer, which=which),
        out_shape=out_shape,
        grid=(t // tm,),
        in_specs=in_specs,
        out_specs=out_specs,
        scratch_shapes=scratch,
        compiler_params=pltpu.CompilerParams(
            dimension_semantics=("arbitrary",), vmem_limit_bytes=VMEM_LIMIT),
        name="ffn_final" if final else ("ffn_first" if first else "ffn"),
    )(*args)


A_WIDTH = A_GROUPS * LANES
SSM_INNER = SSM_HEADS * SSM_HEAD_DIM
SSM_BC = SSM_GROUPS * SSM_STATE
EV_UV = 0
EV_Z = EV_UV + 2 * A_WIDTH
EV_X = EV_Z + SSM_INNER
EV_B = EV_X + SSM_INNER
EV_C = EV_B + SSM_BC
EV_DT = EV_C + SSM_BC


def _gelu_tanh(x):
    c, a = 0.7978845608028654, 0.044715
    hx = 0.5 * x
    return hx + hx * jnp.tanh(x * (c + (c * a) * (x * x)))


def _softplus(x):
    return jnp.maximum(x, 0.0) + jnp.log1p(jnp.exp(-jnp.abs(x)))


def _split3(v):
    hi = v.astype(BF16)
    r = v - hi.astype(F32)
    mid = r.astype(BF16)
    lo = (r - mid.astype(F32)).astype(BF16)
    return hi, mid, lo


def _expand(parts, e):
    n = parts[0].shape[0]
    y = _dot(jnp.concatenate(parts, axis=0), e)
    out = y[0:n]
    for k in range(1, len(parts)):
        out = out + y[k * n:(k + 1) * n]
    return out


def _even_kernel(x_ref, mod_ref, g_ref, win_ref, wdt_ref, wout_ref, ws_ref, bs_ref, gv_ref, wc_ref, bc_ref,
                 dtb_ref, alog_ref, dsk_ref, go_ref, e_ref, *rest, seq_len, n_seq, has_h0, emit_state,
                 n_carry, out_slot):
    rest = list(rest)
    h0_ref = rest.pop(0) if has_h0 else None
    rest = rest[n_carry:]
    o_ref = rest.pop(0)
    st_ref = rest.pop(0) if emit_state else None
    h_ref, mix_ref, xs_ref, b_ref, c_ref, dt_ref, cum_ref, y_ref, state_ref = rest

    rows = seq_len * n_seq
    ck = SSM_CHUNK
    n_chunk = seq_len // ck
    m = mod_ref[0, 0]
    h_ref[...] = (_rms(x_ref[...], g_ref[0]) * (1.0 + m[4:5]) + m[3:4]).astype(BF16)

    vb = _rms(_gelu_tanh(_dot(h_ref[...], win_ref[:, EV_UV + A_WIDTH:EV_Z])), gv_ref[...]).astype(BF16)
    for g in range(A_GROUPS):
        l0 = g * LANES
        u = _gelu_tanh(_dot(h_ref[...], win_ref[:, EV_UV + l0:EV_UV + l0 + LANES]))
        for cc in range(rows // A_CHUNK):
            r0 = cc * A_CHUNK
            s = _dot(ws_ref[g], vb[r0:r0 + A_CHUNK, l0:l0 + LANES]) + bs_ref[g]
            mix_ref[r0:r0 + A_CHUNK, l0:l0 + LANES] = (u[r0:r0 + A_CHUNK] * s).astype(BF16)

    pw = 2 * LANES
    rin = lax.broadcasted_iota(jnp.int32, (rows, pw), 0) % seq_len
    for c0 in range(0, EV_DT - EV_X, pw):
        xbc = _dot(h_ref[...], win_ref[:, EV_X + c0:EV_X + c0 + pw])
        wc = wc_ref[:, c0:c0 + pw]
        prev = jnp.where(rin >= 1, pltpu.roll(xbc, 1, 0), 0.0)
        nxt = jnp.where(rin < seq_len - 1, pltpu.roll(xbc, rows - 1, 0), 0.0)
        xbc = _silu(prev * wc[0:1] + xbc * wc[1:2] + nxt * wc[2:3] + bc_ref[:, c0:c0 + pw])
        if c0 < SSM_INNER:
            xs_ref[:, c0:c0 + pw] = xbc
        elif c0 < SSM_INNER + SSM_BC:
            b_ref[:, c0 - SSM_INNER:c0 - SSM_INNER + pw] = xbc.astype(BF16)
        else:
            c_ref[:, c0 - SSM_INNER - SSM_BC:c0 - SSM_INNER - SSM_BC + pw] = xbc.astype(BF16)

    dt = _softplus(_dot(h_ref[...], wdt_ref[...]) + dtb_ref[...])
    lane = lax.broadcasted_iota(jnp.int32, dt.shape, 1)
    rck = lax.broadcasted_iota(jnp.int32, dt.shape, 0) % ck
    da = jnp.where(lane < 2 * SSM_HEADS, dt * -jnp.exp(alog_ref[...]), 0.0)
    cf = da
    cb = da
    sh = 1
    while sh < ck:
        cf = cf + jnp.where(rck >= sh, pltpu.roll(cf, sh, 0), 0.0)
        cb = cb + jnp.where(rck < ck - sh, pltpu.roll(cb, rows - sh, 0), 0.0)
        sh *= 2
    dt_ref[...] = dt
    cum_ref[...] = jnp.where(lane < SSM_HEADS, cf, cb)

    ii = lax.broadcasted_iota(jnp.int32, (ck, ck), 0)
    jj = lax.broadcasted_iota(jnp.int32, (ck, ck), 1)
    lane_lo = lax.broadcasted_iota(jnp.int32, (ck, LANES), 1) < SSM_HEAD_DIM
    gw = SSM_INNER // SSM_GROUPS
    hpg = SSM_HEADS // SSM_GROUPS

    def chunk_step(r0, direction, slot):
        e = e_ref[:, direction * SSM_INNER:(direction + 1) * SSM_INNER]
        cum_n = cum_ref[pl.ds(r0, ck), :]
        cum_t = cum_n.T
        dt_n = dt_ref[pl.ds(r0, ck), :]
        dt_e = jnp.concatenate(
            [jnp.where(lane_lo,
                       jnp.broadcast_to(dt_n[:, direction * SSM_HEADS + hd:direction * SSM_HEADS + hd + 1], (ck, LANES)),
                       jnp.broadcast_to(dt_n[:, direction * SSM_HEADS + hd + 1:direction * SSM_HEADS + hd + 2], (ck, LANES)))
             for hd in range(0, SSM_HEADS, 2)], axis=1)
        cum_e = _expand(_split3(cum_n), e)
        x_c = xs_ref[pl.ds(r0, ck), :]
        xdt = x_c * dt_e
        if direction == 0:
            tot = cum_e[ck - 1:ck, :]
            tri = ii >= jj
        else:
            tot = cum_e[0:1, :]
            tri = ii <= jj
        ec = jnp.exp(cum_e)
        xd = (xdt * jnp.exp(tot - cum_e)).astype(BF16)
        bm = b_ref[pl.ds(r0, ck), :]
        cm = c_ref[pl.ds(r0, ck), :]
        st = state_ref[slot]
        y_parts = []
        st_parts = []
        for g in range(SSM_GROUPS):
            bg = bm[:, g * SSM_STATE:(g + 1) * SSM_STATE]
            cg = cm[:, g * SSM_STATE:(g + 1) * SSM_STATE]
            cbm = _dot_nt(cg, bg)
            for pr in range(hpg // 2):
                hd = g * hpg + 2 * pr
                l0 = hd * SSM_HEAD_DIM
                sc = []
                for k in range(2):
                    col = direction * SSM_HEADS + hd + k
                    seg = cum_n[:, col:col + 1] - cum_t[col:col + 1, :]
                    sc.append((cbm * jnp.where(tri, jnp.exp(seg), 0.0)).astype(BF16))
                xp = xdt[:, l0:l0 + LANES]
                rhs = jnp.concatenate([jnp.where(lane_lo, xp, 0.0), jnp.where(lane_lo, 0.0, xp)],
                                      axis=0).astype(BF16)
                y_parts.append(_dot(jnp.concatenate(sc, axis=1), rhs))
            y_off = _dot(cg, st[:, g * gw:(g + 1) * gw].astype(BF16))
            y_parts[-2] = y_parts[-2] + y_off[:, 0:LANES] * ec[:, g * gw:g * gw + LANES]
            y_parts[-1] = y_parts[-1] + y_off[:, LANES:gw] * ec[:, g * gw + LANES:(g + 1) * gw]
            st_parts.append(_dot_tn(bg, xd[:, g * gw:(g + 1) * gw]))
        y_ref[direction, pl.ds(r0, ck), :] = jnp.concatenate(y_parts, axis=1)
        state_ref[slot] = st * jnp.exp(tot) + jnp.concatenate(st_parts, axis=1)

    for s in range(n_seq):
        for direction in range(2):
            if has_h0:
                state_ref[2 * s + direction] = h0_ref[s, 0, direction].T
            else:
                state_ref[2 * s + direction] = jnp.zeros(state_ref.shape[1:], F32)

    def scan_body(k, carry):
        for s in range(n_seq):
            for direction in range(2):
                c = k if direction == 0 else n_chunk - 1 - k
                chunk_step(pl.multiple_of(s * seq_len + c * ck, ck), direction, 2 * s + direction)
        return carry

    lax.fori_loop(0, n_chunk, scan_body, 0, unroll=2)
    if emit_state:
        for s in range(n_seq):
            for direction in range(2):
                st_ref[s, out_slot, direction] = state_ref[2 * s + direction].T
        for slot in range(st_ref.shape[1]):
            if slot != out_slot:
                st_ref[:, slot] = jnp.zeros((n_seq,) + st_ref.shape[2:], F32)

    z = _dot(h_ref[...], win_ref[:, EV_Z:EV_X])
    y = y_ref[0] + y_ref[1] + xs_ref[...] * (dsk_ref[0:1] + dsk_ref[1:2])
    mix_ref[:, A_WIDTH:] = _rms(y * _silu(z), go_ref[...]).astype(BF16)
    for c0 in range(0, o_ref.shape[1], pw):
        o_ref[:, c0:c0 + pw] = (x_ref[:, c0:c0 + pw]
                                + m[5:6, c0:c0 + pw] * _dot(mix_ref[...], wout_ref[:, c0:c0 + pw]))


def _even_call(x, mod_l, g, p, h0, *, layer, layer_idx, seq_len, n_seq, row0, n_tiles, mod_set, batch,
               n_layers=1, carry=None):
    t, d = x.shape
    rows = seq_len * n_seq
    tile0 = row0 // rows
    has_h0 = h0 is not None
    emit_state = not has_h0
    hp, ns = SSM_INNER, SSM_STATE
    in_specs = [
        pl.BlockSpec((rows, d), lambda i: (tile0 + i, 0)),
        _mod_spec(mod_l, layer, mod_set),
        _gain_spec(*g),
        _resident(p["w_in"].shape),
        _resident(p["w_dt"].shape),
        _resident(p["w_out"].shape),
        _resident(p["w_s"].shape),
        _resident(p["b_s"].shape),
        _resident(p["g_v"].shape),
        _resident(p["w_conv"].shape),
        _resident(p["b_conv"].shape),
        _resident(p["dt_bias"].shape),
        _resident(p["a_log"].shape),
        _resident(p["d_skip"].shape),
        _resident(p["g_out"].shape),
        _resident(p["expand"].shape),
    ]
    args = [x, mod_l, g[0], p["w_in"], p["w_dt"], p["w_out"], p["w_s"], p["b_s"], p["g_v"], p["w_conv"],
            p["b_conv"], p["dt_bias"], p["a_log"], p["d_skip"], p["g_out"], p["expand"]]
    out_shape = [jax.ShapeDtypeStruct((t, d), F32)]
    out_specs = [pl.BlockSpec((rows, d), lambda i: (tile0 + i, 0))]
    if has_h0:
        in_specs.append(pl.BlockSpec((n_seq, 1, 2, hp, ns), lambda i: (i, layer_idx, 0, 0, 0)))
        args.append(h0)
    aliases = {0: 0}
    if emit_state:
        out_shape.append(jax.ShapeDtypeStruct((batch, n_layers, 2, hp, ns), F32))
        if carry is None:
            out_specs.append(pl.BlockSpec((n_seq, n_layers, 2, hp, ns), lambda i: (i, 0, 0, 0, 0)))
        else:
            out_specs.append(pl.BlockSpec((n_seq, 1, 2, hp, ns), lambda i: (i, layer_idx, 0, 0, 0)))
            in_specs.append(pl.BlockSpec(memory_space=pl.ANY))
            args.append(carry)
            aliases[len(args) - 1] = 1
    out_slot = layer_idx if carry is None else 0
    scratch = [
        pltpu.VMEM((rows, d), BF16),
        pltpu.VMEM((rows, A_WIDTH + SSM_INNER), BF16),
        pltpu.VMEM((rows, SSM_INNER), F32),
        pltpu.VMEM((rows, SSM_BC), BF16),
        pltpu.VMEM((rows, SSM_BC), BF16),
        pltpu.VMEM((rows, LANES), F32),
        pltpu.VMEM((rows, LANES), F32),
        pltpu.VMEM((2, rows, SSM_INNER), F32),
        pltpu.VMEM((2 * n_seq, SSM_STATE, SSM_INNER), F32),
    ]
    res = pl.pallas_call(
        functools.partial(_even_kernel, seq_len=seq_len, n_seq=n_seq, has_h0=has_h0, emit_state=emit_state,
                          n_carry=len(aliases) - 1, out_slot=out_slot),
        out_shape=out_shape,
        grid=(n_tiles,),
        in_specs=in_specs,
        out_specs=out_specs,
        scratch_shapes=scratch,
        input_output_aliases=aliases,
        compiler_params=pltpu.CompilerParams(
            dimension_semantics=("arbitrary",), vmem_limit_bytes=VMEM_LIMIT),
        name="even_mixer_latent" if has_h0 else "even_mixer_context",
    )(*args)
    return res


def _pack_even(w_in, w_out, w_s, b_s, g_v, w_conv, b_conv, dt_bias, a_log, d_skip, g_out):
    d = w_in.shape[0]
    n_dt = 2 * SSM_HEADS
    w_dt = jnp.concatenate([w_in[:, EV_DT:], jnp.zeros((d, LANES - n_dt), F32)], axis=1)

    def narrow(v):
        return jnp.concatenate([v.reshape(1, n_dt), jnp.zeros((1, LANES - n_dt), F32)], axis=1)

    lane_head = jnp.arange(2 * SSM_INNER) // SSM_HEAD_DIM
    expand = (jnp.arange(LANES)[:, None] == lane_head[None, :]).astype(BF16)
    return {
        "w_in": w_in[:, :EV_DT].astype(BF16),
        "w_dt": w_dt.astype(BF16),
        "w_out": w_out.astype(BF16),
        "w_s": w_s.astype(BF16),
        "b_s": jnp.broadcast_to(b_s[:, :, None], b_s.shape + (LANES,)),
        "g_v": g_v.reshape(1, -1),
        "w_conv": w_conv,
        "b_conv": b_conv.reshape(1, -1),
        "dt_bias": narrow(dt_bias),
        "a_log": narrow(a_log),
        "d_skip": jnp.repeat(d_skip, SSM_HEAD_DIM, axis=1),
        "g_out": g_out.reshape(1, -1),
        "expand": expand,
    }


MLA_Q_RANK = 384
MLA_KV_RANK = 256
CONV_WIDTH = 512
HEAD_W = LANES
ROPE_L0 = MLA_NOPE
OD_CQ = 0
OD_CKV = OD_CQ + MLA_Q_RANK
OD_LAT = OD_CKV + MLA_KV_RANK
CONV_PAD = 16
CONV_RB = 64
ATT_QB = 256


def _odd_kernel(x_ref, mod_ref, g_ref, win_ref, wag_ref, wkr_ref, wuq_ref, wuk_ref, wuv_ref, wout_ref, gcq_ref, gckv_ref,
                wdw_ref, bdw_ref, gln_ref, bln_ref, *rest, seq_len, n_seq, past, n_carry, out_slot):
    rest = list(rest)
    has_cache = past > 0
    if has_cache:
        cckv_ref, ckr_ref, cos_ref, sin_ref = rest[:4]
        o_ref = rest[4]
        rest = rest[5:]
    else:
        rest = rest[n_carry:]
        o_ref, ckv_out_ref, kr_out_ref = rest[:3]
        rest = rest[3:]
    h_ref, cq_ref, ckv_ref, kr_ref, mix_ref, q_ref, k_ref, v_ref, oh_ref, pad_ref, cout_ref = rest

    nh = MLA_HEADS
    hw = nh * MLA_V
    att_qb = min(ATT_QB, seq_len)
    first_head_lanes = lax.broadcasted_iota(jnp.int32, (att_qb, HEAD_W), 1) < MLA_V
    n_keys = past + seq_len
    scale = (MLA_NOPE + MLA_ROPE) ** -0.5
    m = mod_ref[0, 0]
    h_ref[...] = (_rms(x_ref[...], g_ref[0]) * (1.0 + m[4:5]) + m[3:4]).astype(BF16)

    cq_ref[...] = _rms(_dot(h_ref[...], win_ref[:, OD_CQ:OD_CKV]), gcq_ref[...]).astype(BF16)
    ckv = _rms(_dot(h_ref[...], win_ref[:, OD_CKV:OD_LAT]), gckv_ref[...])
    ckv_ref[...] = ckv.astype(BF16)
    kr = _dot(h_ref[...], wkr_ref[:, 0:HEAD_W])
    if has_cache:
        cos = jnp.concatenate([cos_ref[...]] * n_seq, axis=0)
        sin = jnp.concatenate([sin_ref[...]] * n_seq, axis=0)
        kr = kr * cos + _dot(h_ref[...], wkr_ref[:, HEAD_W:2 * HEAD_W]) * sin
    else:
        for s in range(n_seq):
            ckv_out_ref[s, out_slot] = ckv[s * seq_len:(s + 1) * seq_len]
            kr_out_ref[s, out_slot] = kr[s * seq_len:(s + 1) * seq_len, ROPE_L0:ROPE_L0 + MLA_ROPE]
        for slot in range(ckv_out_ref.shape[1]):
            if slot != out_slot:
                ckv_out_ref[:, slot] = jnp.zeros((n_seq,) + ckv_out_ref.shape[2:], F32)
                kr_out_ref[:, slot] = jnp.zeros((n_seq,) + kr_out_ref.shape[2:], F32)
    kr_ref[...] = kr

    n_pair = nh // 2

    def twice(a):
        return jnp.concatenate([a, a], axis=1)

    def q_body(pr, carry):
        q = _dot(cq_ref[...], wuq_ref[pr])
        if has_cache:
            q = q * twice(cos) + _dot(cq_ref[...], wuq_ref[n_pair + pr]) * twice(sin)
        q_ref[pr] = q.astype(BF16)
        return carry

    lax.fori_loop(0, n_pair, q_body, 0, unroll=4)

    n_qb = seq_len // att_qb
    for s in range(n_seq):
        r0 = s * seq_len

        def kv_body(pr, carry, s=s, r0=r0):
            lat = ckv_ref[r0:r0 + seq_len]
            k_ref[pr, past:n_keys] = (_dot(lat, wuk_ref[pr]) + twice(kr_ref[r0:r0 + seq_len])).astype(BF16)
            v_ref[pr, past:n_keys] = _dot(lat, wuv_ref[pr]).astype(BF16)
            if has_cache:
                lat_c = cckv_ref[s, 0].astype(BF16)
                k_ref[pr, 0:past] = (_dot(lat_c, wuk_ref[pr]) + twice(ckr_ref[s, 0])).astype(BF16)
                v_ref[pr, 0:past] = _dot(lat_c, wuv_ref[pr]).astype(BF16)
            return carry

        lax.fori_loop(0, n_pair, kv_body, 0, unroll=4)

        def att_body(it, carry, r0=r0):
            pr = it // n_qb
            q0 = pl.multiple_of(r0 + (it % n_qb) * att_qb, att_qb)
            outs = []
            for l0 in range(0, 2 * HEAD_W, HEAD_W):
                sc = _dot_nt(q_ref[pr, pl.ds(q0, att_qb), l0:l0 + HEAD_W], k_ref[pr, :, l0:l0 + HEAD_W]) * scale
                p = jnp.exp(sc - jnp.max(sc, axis=-1, keepdims=True))
                den = jnp.sum(p, axis=-1, keepdims=True)
                outs.append(_dot(p.astype(BF16), v_ref[pr]) / den)
            oh_ref[pr, pl.ds(q0, att_qb), :] = jnp.where(first_head_lanes, outs[0], outs[1]).astype(BF16)
            return carry

        lax.fori_loop(0, n_pair * n_qb, att_body, 0, unroll=4)

    for pr in range(n_pair):
        mix_ref[:, pr * HEAD_W:(pr + 1) * HEAD_W] = oh_ref[pr]

    pw = 2 * LANES
    n_lt = CONV_WIDTH // LANES
    nv = CONV_RB // 8
    zpad = jnp.zeros((n_lt, CONV_PAD, LANES), F32)
    for s in range(n_seq):
        r0 = s * seq_len
        pad_ref[:, 0:CONV_PAD] = zpad
        pad_ref[:, CONV_PAD + seq_len:2 * CONV_PAD + seq_len] = zpad
        for c0 in range(0, CONV_WIDTH, pw):
            a = _dot(h_ref[r0:r0 + seq_len], wag_ref[:, c0:c0 + pw])
            gt = _dot(h_ref[r0:r0 + seq_len], wag_ref[:, CONV_WIDTH + c0:CONV_WIDTH + c0 + pw])
            glu = a * _sigmoid(gt)
            for l0 in range(0, pw, LANES):
                pad_ref[(c0 + l0) // LANES, CONV_PAD:CONV_PAD + seq_len] = glu[:, l0:l0 + LANES]

        def conv_body(b, carry, r0=r0):
            base = pl.multiple_of(b * CONV_RB, CONV_RB)
            for lt in range(n_lt):
                l0 = lt * LANES
                accs = [jnp.broadcast_to(bdw_ref[:, l0:l0 + LANES], (8, LANES))] * nv
                for k in range(CONV_K):
                    wk = jnp.broadcast_to(wdw_ref[k:k + 1, l0:l0 + LANES], (8, LANES))
                    off = k + CONV_PAD - CONV_K // 2
                    accs = [accs[v] + wk * pad_ref[lt, pl.ds(base + v + off, 8, stride=nv), :] for v in range(nv)]
                for v in range(nv):
                    cout_ref[lt, pl.ds(base + v, 8, stride=nv), :] = accs[v]
            acc = jnp.concatenate([cout_ref[lt, pl.ds(base, CONV_RB), :] for lt in range(n_lt)], axis=1)
            dc = acc - jnp.mean(acc, axis=-1, keepdims=True)
            var = jnp.mean(dc * dc, axis=-1, keepdims=True)
            dn = dc * lax.rsqrt(var + EPS) * gln_ref[...] + bln_ref[...]
            mix_ref[pl.ds(r0 + base, CONV_RB), hw:] = _silu(dn).astype(BF16)
            return carry

        lax.fori_loop(0, seq_len // CONV_RB, conv_body, 0, unroll=4)

    for c0 in range(0, o_ref.shape[1], pw):
        o_ref[:, c0:c0 + pw] = (x_ref[:, c0:c0 + pw]
                                + m[5:6, c0:c0 + pw] * _dot(mix_ref[...], wout_ref[:, c0:c0 + pw]))


def _odd_call(x, mod_l, g, p, cache, *, layer, layer_idx, seq_len, n_seq, row0, n_tiles, mod_set, batch,
              n_layers=1, carry=None):
    t, d = x.shape
    rows = seq_len * n_seq
    tile0 = row0 // rows
    has_cache = cache is not None
    past = cache[0].shape[2] if has_cache else 0
    names = ["w_in", "w_ag", "w_kr", "w_uq", "w_uk", "w_uv", "w_out", "g_cq", "g_ckv", "w_dw", "b_dw", "g_ln", "b_ln"]
    in_specs = [
        pl.BlockSpec((rows, d), lambda i: (tile0 + i, 0)),
        _mod_spec(mod_l, layer, mod_set),
        _gain_spec(*g),
    ] + [_resident(p[k].shape) for k in names]
    args = [x, mod_l, g[0]] + [p[k] for k in names]
    out_shape = [jax.ShapeDtypeStruct((t, d), F32)]
    out_specs = [pl.BlockSpec((rows, d), lambda i: (tile0 + i, 0))]
    aliases = {0: 0}
    if has_cache:
        cckv, ckr, cos, sin = cache
        in_specs += [
            pl.BlockSpec((n_seq, 1, past, MLA_KV_RANK), lambda i: (i, layer_idx, 0, 0)),
            pl.BlockSpec((n_seq, 1, past, HEAD_W), lambda i: (i, layer_idx, 0, 0)),
            _resident(cos.shape),
            _resident(sin.shape),
        ]
        args += [cckv, ckr, cos, sin]
    else:
        out_shape += [jax.ShapeDtypeStruct((batch, n_layers, seq_len, MLA_KV_RANK), F32),
                      jax.ShapeDtypeStruct((batch, n_layers, seq_len, MLA_ROPE), F32)]
        if carry is None:
            out_specs += [pl.BlockSpec((n_seq, n_layers, seq_len, MLA_KV_RANK), lambda i: (i, 0, 0, 0)),
                          pl.BlockSpec((n_seq, n_layers, seq_len, MLA_ROPE), lambda i: (i, 0, 0, 0))]
        else:
            out_specs += [pl.BlockSpec((n_seq, 1, seq_len, MLA_KV_RANK), lambda i: (i, layer_idx, 0, 0)),
                          pl.BlockSpec((n_seq, 1, seq_len, MLA_ROPE), lambda i: (i, layer_idx, 0, 0))]
            for k, arr in enumerate(carry):
                in_specs.append(pl.BlockSpec(memory_space=pl.ANY))
                args.append(arr)
                aliases[len(args) - 1] = 1 + k
    n_keys = past + seq_len
    scratch = [
        pltpu.VMEM((rows, d), BF16),
        pltpu.VMEM((rows, MLA_Q_RANK), BF16),
        pltpu.VMEM((rows, MLA_KV_RANK), BF16),
        pltpu.VMEM((rows, HEAD_W), F32),
        pltpu.VMEM((rows, MLA_HEADS * MLA_V + CONV_WIDTH), BF16),
        pltpu.VMEM((MLA_HEADS // 2, rows, 2 * HEAD_W), BF16),
        pltpu.VMEM((MLA_HEADS // 2, n_keys, 2 * HEAD_W), BF16),
        pltpu.VMEM((MLA_HEADS // 2, n_keys, 2 * MLA_V), BF16),
        pltpu.VMEM((MLA_HEADS // 2, rows, 2 * MLA_V), BF16),
        pltpu.VMEM((CONV_WIDTH // LANES, seq_len + 2 * CONV_PAD, LANES), F32),
        pltpu.VMEM((CONV_WIDTH // LANES, seq_len, LANES), F32),
    ]
    return pl.pallas_call(
        functools.partial(_odd_kernel, seq_len=seq_len, n_seq=n_seq, past=past, n_carry=len(aliases) - 1,
                          out_slot=layer_idx if (not has_cache and carry is None) else 0),
        out_shape=out_shape,
        grid=(n_tiles,),
        in_specs=in_specs,
        out_specs=out_specs,
        scratch_shapes=scratch,
        input_output_aliases=aliases,
        compiler_params=pltpu.CompilerParams(
            dimension_semantics=("arbitrary",), vmem_limit_bytes=VMEM_LIMIT),
        name="odd_mixer_latent" if has_cache else "odd_mixer_context",
    )(*args)


def _head_tiles(w, per_head, start, width, lane0):
    k = w.shape[0]
    blk = w.reshape(k, MLA_HEADS, per_head)[:, :, start:start + width]
    out = jnp.zeros((k, MLA_HEADS, HEAD_W), w.dtype).at[:, :, lane0:lane0 + width].set(blk)
    return out.reshape(k, MLA_HEADS // 2, 2 * HEAD_W).transpose(1, 0, 2)


def _rope_swap(w):
    q = MLA_ROPE // 4
    idx = jnp.concatenate([jnp.arange(q, 2 * q), jnp.arange(0, q), jnp.arange(3 * q, 4 * q), jnp.arange(2 * q, 3 * q)])
    return w[..., idx]


def _pack_odd(w_in, w_out, g_cq, w_uq, g_ckv, w_ukv, w_dw, b_dw, g_ln, b_ln):
    d = w_in.shape[0]
    o1 = MLA_Q_RANK
    o2 = o1 + MLA_KV_RANK
    o3 = o2 + MLA_ROPE
    w_kr = w_in[:, o2:o3]

    def kr_tile(w):
        return jnp.zeros((d, HEAD_W), F32).at[:, ROPE_L0:ROPE_L0 + MLA_ROPE].set(w)

    w_kr_p = jnp.concatenate([kr_tile(w_kr), kr_tile(_rope_swap(w_kr))], axis=1)
    qh = MLA_NOPE + MLA_ROPE
    rank = w_uq.shape[0]
    w_q3 = w_uq.reshape(rank, MLA_HEADS, qh)
    w_q_sw = jnp.concatenate([jnp.zeros((rank, MLA_HEADS, MLA_NOPE), F32), _rope_swap(w_q3[:, :, MLA_NOPE:])],
                             axis=2).reshape(rank, MLA_HEADS * qh)
    w_uq_p = jnp.concatenate([_head_tiles(w_uq, qh, 0, qh, 0),
                              _head_tiles(w_q_sw, qh, MLA_NOPE, MLA_ROPE, ROPE_L0)], axis=0)
    kvh = MLA_NOPE + MLA_V
    w_uk_p = _head_tiles(w_ukv, kvh, 0, MLA_NOPE, 0)
    rank_kv = w_ukv.shape[0]
    w_uv_p = (w_ukv.reshape(rank_kv, MLA_HEADS, kvh)[:, :, MLA_NOPE:]
              .reshape(rank_kv, MLA_HEADS // 2, 2 * MLA_V).transpose(1, 0, 2))
    return {
        "w_in": w_in[:, :o2].astype(BF16),
        "w_ag": w_in[:, o3:].astype(BF16),
        "w_kr": w_kr_p.astype(BF16),
        "w_uq": w_uq_p.astype(BF16),
        "w_uk": w_uk_p.astype(BF16),
        "w_uv": w_uv_p.astype(BF16),
        "w_out": w_out.astype(BF16),
        "g_cq": g_cq.reshape(1, -1),
        "g_ckv": g_ckv.reshape(1, -1),
        "w_dw": w_dw,
        "b_dw": b_dw.reshape(1, -1),
        "g_ln": g_ln.reshape(1, -1),
        "b_ln": b_ln.reshape(1, -1),
    }


def _rope_tables(length):
    n_freq = MLA_ROPE // 4
    pos = jnp.arange(length)
    row = (pos // GRID_W).astype(F32)
    col = (pos % GRID_W).astype(F32)
    freqs = ROPE_THETA ** (-jnp.arange(n_freq, dtype=F32) / n_freq)
    ar = row[:, None] * freqs
    ac = col[:, None] * freqs
    cos = jnp.concatenate([jnp.cos(ar), jnp.cos(ar), jnp.cos(ac), jnp.cos(ac)], axis=1)
    sin = jnp.concatenate([-jnp.sin(ar), jnp.sin(ar), -jnp.sin(ac), jnp.sin(ac)], axis=1)
    tail = HEAD_W - ROPE_L0 - MLA_ROPE
    cos_t = jnp.concatenate([jnp.ones((length, ROPE_L0), F32), cos, jnp.zeros((length, tail), F32)], axis=1)
    sin_t = jnp.concatenate([jnp.zeros((length, ROPE_L0), F32), sin, jnp.zeros((length, tail), F32)], axis=1)
    return cos_t, sin_t


def kernel(x_prompt, x_sample, state_ssd, cache_mla_ckv, cache_mla_krope, c, c_ctx, w_mod, b_mod, g_norm, w_ff_gu, w_ff_down, w_in_even, w_out_even, w_spatial, b_spatial, g_gmlp_v, w_conv_ssm, b_conv_ssm, dt_bias, a_log, d_skip, g_ssm_out, w_in_odd, w_out_odd, g_cq, w_uq, g_ckv, w_ukv, w_dwconv, b_dwconv, g_conv_ln, b_conv_ln, g_final):
    bp, seq, d = x_prompt.shape
    db, dec_seq, _ = x_sample.shape
    n_ctx = bp * seq
    cond_t = jnp.concatenate([c_ctx[None], c], axis=0).T
    mod = _mod_call(cond_t, w_mod, b_mod).reshape(w_mod.shape[0], 1 + db, N_MOD, d)
    depth = w_mod.shape[0]
    ffn = functools.partial(_ffn_call, tm=512, n_ctx_rows=n_ctx, dec_seq=dec_seq)
    ctx_seqs = 4
    ctx_tile = dict(seq_len=seq, n_seq=ctx_seqs, row0=0, n_tiles=bp // ctx_seqs, mod_set=lambda i: 0, batch=bp)
    lat_tile = dict(seq_len=dec_seq, n_seq=1, row0=n_ctx, n_tiles=db, mod_set=lambda i: 1 + i, batch=bp)
    h0 = state_ssd.reshape(db, state_ssd.shape[1], 2, SSM_INNER, SSM_STATE)
    ckr_tile = jnp.pad(cache_mla_krope, ((0, 0), (0, 0), (0, 0), (ROPE_L0, HEAD_W - ROPE_L0 - MLA_ROPE)))
    cos_t, sin_t = _rope_tables(dec_seq)
    n_even, n_odd = (depth + 1) // 2, depth // 2
    new_state = new_cache = None
    xs = [x_prompt.reshape(n_ctx, d), x_sample.reshape(db * dec_seq, d)]
    n_gain = g_norm.shape[1]
    gains = g_norm.reshape(depth * n_gain, 1, d)
    for l in range(depth):
        i = l // 2
        last = l == depth - 1
        x = ffn(xs, mod, (gains, l * n_gain), w_ff_gu, w_ff_down, None, layer=l, which=0, mrow=0)
        g_mix = (gains, l * n_gain + 1)
        if l % 2 == 0:
            pe = _pack_even(w_in_even[i], w_out_even[i], w_spatial[i], b_spatial[i], g_gmlp_v[i], w_conv_ssm[i],
                            b_conv_ssm[i], dt_bias[i], a_log[i], d_skip[i], g_ssm_out[i])
            x, new_state = _even_call(x, mod, g_mix, pe, None, layer=l, layer_idx=i, n_layers=n_even,
                                      carry=new_state, **ctx_tile)
            (x,) = _even_call(x, mod, g_mix, pe, h0, layer=l, layer_idx=i, **lat_tile)
        else:
            po = _pack_odd(w_in_odd[i], w_out_odd[i], g_cq[i], w_uq[i], g_ckv[i], w_ukv[i], w_dwconv[i],
                           b_dwconv[i], g_conv_ln[i], b_conv_ln[i])
            x, *new_cache = _odd_call(x, mod, g_mix, po, None, layer=l, layer_idx=i, n_layers=n_odd,
                                      carry=new_cache, **ctx_tile)
            (x,) = _odd_call(x, mod, g_mix, po, (cache_mla_ckv, ckr_tile, cos_t, sin_t),
                             layer=l, layer_idx=i, **lat_tile)
        res = ffn([x], mod, (gains, l * n_gain + 2), w_ff_gu, w_ff_down, g_final if last else None,
                  layer=l, which=1, mrow=6)
        xs = [res]
    y_ctx, y_lat = res
    new_state = new_state.reshape(bp, n_even, 2, SSM_HEADS, SSM_HEAD_DIM, SSM_STATE)
    return (y_ctx.reshape(bp, seq, d), y_lat.reshape(db, dec_seq, d), new_state, new_cache[0], new_cache[1])
```

```python
import functools

import jax
import jax.numpy as jnp
from jax import lax
from jax.experimental import pallas as pl
from jax.experimental.pallas import tpu as pltpu

F32 = jnp.float32
BF16 = jnp.bfloat16

EPS = 1e-6
N_MOD = 9
LANES = 128
A_GROUPS = 4
A_CHUNK = 128
SSM_HEADS = 8
SSM_HEAD_DIM = 64
SSM_GROUPS = 2
SSM_STATE = 128
SSM_CHUNK = 128
MLA_HEADS = 8
MLA_NOPE = 64
MLA_ROPE = 32
MLA_V = 64
CONV_K = 31
GRID_W = 64
ROPE_THETA = 10000.0

VMEM_LIMIT = 60 * 1024 * 1024


def _dot(a, b):
    return jnp.dot(a, b, preferred_element_type=F32)


def _dot_nt(a, b):
    return lax.dot_general(a, b, (((1,), (1,)), ((), ())), preferred_element_type=F32)


def _dot_tn(a, b):
    return lax.dot_general(a, b, (((0,), (0,)), ((), ())), preferred_element_type=F32)


def _rms(x, g):
    return x * lax.rsqrt(jnp.mean(x * x, axis=-1, keepdims=True) + EPS) * g


def _sigmoid(x):
    return 1.0 / (1.0 + jnp.exp(-x))


def _silu(x):
    return x * _sigmoid(x)


def _mod_index(tile_rows, n_ctx_rows, dec_seq):
    def index(i):
        return jnp.maximum((i * tile_rows - n_ctx_rows) // dec_seq + 1, 0)
    return index


def _mod_spec(mod_all, layer, set_of_tile):
    return pl.BlockSpec((1, 1) + mod_all.shape[2:], lambda i: (layer, set_of_tile(i), 0, 0))


def _gain_spec(g_all, idx):
    return pl.BlockSpec((1,) + g_all.shape[1:], lambda i: (idx, 0, 0), pipeline_mode=pl.Buffered(1))


def _layer_window(w_all, layer_idx, n_cols):
    return pl.BlockSpec((1, w_all.shape[1], n_cols), lambda i: (layer_idx, 0, 0), pipeline_mode=pl.Buffered(1))


def _resident(shape):
    nd = len(shape)
    return pl.BlockSpec(shape, lambda i: (0,) * nd, pipeline_mode=pl.Buffered(1))


def _mod_kernel(ct_ref, w_ref, b_ref, o_ref, *, n_sets):
    w = w_ref[0]
    s = _silu(ct_ref[...])
    rows = [jnp.sum(w * s[:, r:r + 1], axis=0, keepdims=True) for r in range(n_sets)]
    o_ref[0] = jnp.concatenate(rows, axis=0) + b_ref[0]


def _mod_call(cond_t, w_mod, b_mod):
    depth, d, n = w_mod.shape
    n_sets = cond_t.shape[1]
    tn = n // 4
    return pl.pallas_call(
        functools.partial(_mod_kernel, n_sets=n_sets),
        out_shape=jax.ShapeDtypeStruct((depth, n_sets, n), F32),
        grid=(depth, n // tn),
        in_specs=[
            pl.BlockSpec((d, n_sets), lambda l, j: (0, 0)),
            pl.BlockSpec((1, d, tn), lambda l, j: (l, 0, j)),
            pl.BlockSpec((1, 1, tn), lambda l, j: (l, 0, j)),
        ],
        out_specs=pl.BlockSpec((1, n_sets, tn), lambda l, j: (l, 0, j)),
        compiler_params=pltpu.CompilerParams(
            dimension_semantics=("arbitrary", "arbitrary"), vmem_limit_bytes=VMEM_LIMIT),
        name="adaln_mod",
    )(cond_t, w_mod, b_mod.reshape(depth, 1, n))


FFN_CHUNK = 2 * LANES


def _ffn_kernel(*refs, mrow, d_ff, ctx_tiles, first, final, layer, which):
    refs = list(refs)
    n_x = 2 if first else 1
    x_refs, (mod_ref, g_ref, wgu_hbm, wd_hbm), refs = refs[:n_x], refs[n_x:n_x + 4], refs[n_x + 4:]
    wgu_ref, wd_ref, sem = refs[-3:]
    i = pl.program_id(0)

    def weight_copies(c0):
        cols, rows = pl.ds(c0, FFN_CHUNK), pl.ds(c0, FFN_CHUNK)
        ucols = pl.ds(d_ff + c0, FFN_CHUNK)
        k = c0 // FFN_CHUNK
        return (pltpu.make_async_copy(wgu_hbm.at[layer, which, :, cols], wgu_ref.at[:, cols], sem.at[0, k]),
                pltpu.make_async_copy(wgu_hbm.at[layer, which, :, ucols], wgu_ref.at[:, ucols], sem.at[1, k]),
                pltpu.make_async_copy(wd_hbm.at[layer, which, rows, :], wd_ref.at[rows, :], sem.at[2, k]))

    def run(first_step):
        if first and first_step:
            x = x_refs[0][...]
        elif first:
            x = jnp.where(i < ctx_tiles, x_refs[0][...], x_refs[1][...])
        else:
            x = x_refs[0][...]
        m = mod_ref[0, 0]
        shift, scale, gate = m[mrow:mrow + 1], m[mrow + 1:mrow + 2], m[mrow + 2:mrow + 3]
        h = (_rms(x, g_ref[0]) * (1.0 + scale) + shift).astype(BF16)
        acc = jnp.zeros(x.shape, F32)
        for c0 in range(0, d_ff, FFN_CHUNK):
            if first_step:
                for cp in weight_copies(c0):
                    cp.wait()
            g = _dot(h, wgu_ref[:, c0:c0 + FFN_CHUNK].astype(BF16))
            u = _dot(h, wgu_ref[:, d_ff + c0:d_ff + c0 + FFN_CHUNK].astype(BF16))
            a = (_silu(g) * u).astype(BF16)
            acc = acc + _dot(a, wd_ref[c0:c0 + FFN_CHUNK, :].astype(BF16))
        out = x + 0.5 * gate * acc
        if not final:
            refs[0][...] = out
            return
        gf_ref, yc_ref, yl_ref = refs[:3]
        y = _rms(out, gf_ref[...])
        if first_step:
            yc_ref[...] = y
            return

        @pl.when(i < ctx_tiles)
        def _():
            yc_ref[...] = y

        @pl.when(i >= ctx_tiles)
        def _():
            yl_ref[...] = y

    @pl.when(i == 0)
    def _():
        for c0 in range(0, d_ff, FFN_CHUNK):
            for cp in weight_copies(c0):
                cp.start()
        run(True)

    @pl.when(i > 0)
    def _():
        run(False)


def _ffn_call(xs, mod_l, g, w_gu, w_d, g_final, *, layer, which, mrow, tm, n_ctx_rows, dec_seq):
    first = len(xs) == 2
    final = g_final is not None
    d = xs[0].shape[1]
    t = sum(a.shape[0] for a in xs)
    d_ff = w_d.shape[2]
    ctx_tiles = n_ctx_rows // tm

    def ctx_map(i):
        return (jnp.minimum(i, ctx_tiles - 1), 0)

    def lat_map(i):
        return (jnp.maximum(i - ctx_tiles, 0), 0)

    x_specs = ([pl.BlockSpec((tm, d), ctx_map), pl.BlockSpec((tm, d), lat_map)] if first
               else [pl.BlockSpec((tm, d), lambda i: (i, 0))])
    in_specs = x_specs + [
        _mod_spec(mod_l, layer, _mod_index(tm, n_ctx_rows, dec_seq)),
        _gain_spec(*g),
        pl.BlockSpec(memory_space=pl.ANY),
        pl.BlockSpec(memory_space=pl.ANY),
    ]
    scratch = [pltpu.VMEM(w_gu.shape[2:], F32), pltpu.VMEM(w_d.shape[2:], F32),
               pltpu.SemaphoreType.DMA((3, d_ff // FFN_CHUNK))]
    args = list(xs) + [mod_l, g[0], w_gu, w_d]
    if final:
        in_specs.append(_resident((1, d)))
        args.append(g_final.reshape(1, d))
        out_shape = (jax.ShapeDtypeStruct((n_ctx_rows, d), F32), jax.ShapeDtypeStruct((t - n_ctx_rows, d), F32))
        out_specs = (pl.BlockSpec((tm, d), ctx_map), pl.BlockSpec((tm, d), lat_map))
    else:
        out_shape = jax.ShapeDtypeStruct((t, d), F32)
        out_specs = pl.BlockSpec((tm, d), lambda i: (i, 0))
    return pl.pallas_call(
        functools.partial(_ffn_kernel, mrow=mrow, d_ff=d_ff, ctx_tiles=ctx_tiles, first=first, final=final,
                          layer=layer, which=which),
        out_shape=out_shape,
        grid=(t // tm,),
        in_specs=in_specs,
        out_specs=out_specs,
        scratch_shapes=scratch,
        compiler_params=pltpu.CompilerParams(
            dimension_semantics=("arbitrary",), vmem_limit_bytes=VMEM_LIMIT),
        name="ffn_final" if final else ("ffn_first" if first else "ffn"),
    )(*args)


A_WIDTH = A_GROUPS * LANES
SSM_INNER = SSM_HEADS * SSM_HEAD_DIM
SSM_BC = SSM_GROUPS * SSM_STATE
EV_UV = 0
EV_Z = EV_UV + 2 * A_WIDTH
EV_X = EV_Z + SSM_INNER
EV_B = EV_X + SSM_INNER
EV_C = EV_B + SSM_BC
EV_DT = EV_C + SSM_BC


def _gelu_tanh(x):
    c, a = 0.7978845608028654, 0.044715
    hx = 0.5 * x
    return hx + hx * jnp.tanh(x * (c + (c * a) * (x * x)))


def _softplus(x):
    return jnp.maximum(x, 0.0) + jnp.log1p(jnp.exp(-jnp.abs(x)))


def _split3(v):
    hi = v.astype(BF16)
    r = v - hi.astype(F32)
    mid = r.astype(BF16)
    lo = (r - mid.astype(F32)).astype(BF16)
    return hi, mid, lo


def _expand(parts, e):
    n = parts[0].shape[0]
    y = _dot(jnp.concatenate(parts, axis=0), e)
    out = y[0:n]
    for k in range(1, len(parts)):
        out = out + y[k * n:(k + 1) * n]
    return out


def _even_kernel(x_ref, mod_ref, g_ref, win_ref, wdt_ref, wout_ref, ws_ref, bs_ref, gv_ref, wc_ref, bc_ref,
                 dtb_ref, alog_ref, dsk_ref, go_ref, e_ref, *rest, seq_len, n_seq, has_h0, emit_state,
                 n_carry, out_slot):
    rest = list(rest)
    h0_ref = rest.pop(0) if has_h0 else None
    rest = rest[n_carry:]
    o_ref = rest.pop(0)
    st_ref = rest.pop(0) if emit_state else None
    h_ref, mix_ref, xs_ref, b_ref, c_ref, dt_ref, cum_ref, y_ref, state_ref = rest

    rows = seq_len * n_seq
    ck = SSM_CHUNK
    n_chunk = seq_len // ck
    m = mod_ref[0, 0]
    h_ref[...] = (_rms(x_ref[...], g_ref[0]) * (1.0 + m[4:5]) + m[3:4]).astype(BF16)

    def w_in(c0, c1):
        return win_ref[0, :, c0:c1].astype(BF16)

    vb = _rms(_gelu_tanh(_dot(h_ref[...], w_in(EV_UV + A_WIDTH, EV_Z))), gv_ref[...]).astype(BF16)
    for g in range(A_GROUPS):
        l0 = g * LANES
        u = _gelu_tanh(_dot(h_ref[...], w_in(EV_UV + l0, EV_UV + l0 + LANES)))
        for cc in range(rows // A_CHUNK):
            r0 = cc * A_CHUNK
            s = _dot(ws_ref[g], vb[r0:r0 + A_CHUNK, l0:l0 + LANES]) + bs_ref[g]
            mix_ref[r0:r0 + A_CHUNK, l0:l0 + LANES] = (u[r0:r0 + A_CHUNK] * s).astype(BF16)

    pw = 2 * LANES
    rin = lax.broadcasted_iota(jnp.int32, (rows, pw), 0) % seq_len
    for c0 in range(0, EV_DT - EV_X, pw):
        xbc = _dot(h_ref[...], w_in(EV_X + c0, EV_X + c0 + pw))
        wc = wc_ref[:, c0:c0 + pw]
        prev = jnp.where(rin >= 1, pltpu.roll(xbc, 1, 0), 0.0)
        nxt = jnp.where(rin < seq_len - 1, pltpu.roll(xbc, rows - 1, 0), 0.0)
        xbc = _silu(prev * wc[0:1] + xbc * wc[1:2] + nxt * wc[2:3] + bc_ref[:, c0:c0 + pw])
        if c0 < SSM_INNER:
            xs_ref[:, c0:c0 + pw] = xbc
        elif c0 < SSM_INNER + SSM_BC:
            b_ref[:, c0 - SSM_INNER:c0 - SSM_INNER + pw] = xbc.astype(BF16)
        else:
            c_ref[:, c0 - SSM_INNER - SSM_BC:c0 - SSM_INNER - SSM_BC + pw] = xbc.astype(BF16)

    dt = _softplus(_dot(h_ref[...], wdt_ref[...]) + dtb_ref[...])
    lane = lax.broadcasted_iota(jnp.int32, dt.shape, 1)
    rck = lax.broadcasted_iota(jnp.int32, dt.shape, 0) % ck
    da = jnp.where(lane < 2 * SSM_HEADS, dt * -jnp.exp(alog_ref[...]), 0.0)
    cf = da
    cb = da
    sh = 1
    while sh < ck:
        cf = cf + jnp.where(rck >= sh, pltpu.roll(cf, sh, 0), 0.0)
        cb = cb + jnp.where(rck < ck - sh, pltpu.roll(cb, rows - sh, 0), 0.0)
        sh *= 2
    dt_ref[...] = dt
    cum_ref[...] = jnp.where(lane < SSM_HEADS, cf, cb)

    ii = lax.broadcasted_iota(jnp.int32, (ck, ck), 0)
    jj = lax.broadcasted_iota(jnp.int32, (ck, ck), 1)
    lane_lo = lax.broadcasted_iota(jnp.int32, (ck, LANES), 1) < SSM_HEAD_DIM
    gw = SSM_INNER // SSM_GROUPS
    hpg = SSM_HEADS // SSM_GROUPS

    def chunk_step(r0, direction, slot):
        e = e_ref[:, direction * SSM_INNER:(direction + 1) * SSM_INNER]
        cum_n = cum_ref[pl.ds(r0, ck), :]
        cum_t = cum_n.T
        dt_n = dt_ref[pl.ds(r0, ck), :]
        dt_e = jnp.concatenate(
            [jnp.where(lane_lo,
                       jnp.broadcast_to(dt_n[:, direction * SSM_HEADS + hd:direction * SSM_HEADS + hd + 1], (ck, LANES)),
                       jnp.broadcast_to(dt_n[:, direction * SSM_HEADS + hd + 1:direction * SSM_HEADS + hd + 2], (ck, LANES)))
             for hd in range(0, SSM_HEADS, 2)], axis=1)
        cum_e = _expand(_split3(cum_n), e)
        x_c = xs_ref[pl.ds(r0, ck), :]
        xdt = x_c * dt_e
        if direction == 0:
            tot = cum_e[ck - 1:ck, :]
            tri = ii >= jj
        else:
            tot = cum_e[0:1, :]
            tri = ii <= jj
        ec = jnp.exp(cum_e)
        xd = (xdt * jnp.exp(tot - cum_e)).astype(BF16)
        bm = b_ref[pl.ds(r0, ck), :]
        cm = c_ref[pl.ds(r0, ck), :]
        st = state_ref[slot]
        y_parts = []
        st_parts = []
        for g in range(SSM_GROUPS):
            bg = bm[:, g * SSM_STATE:(g + 1) * SSM_STATE]
            cg = cm[:, g * SSM_STATE:(g + 1) * SSM_STATE]
            cbm = _dot_nt(cg, bg)
            for pr in range(hpg // 2):
                hd = g * hpg + 2 * pr
                l0 = hd * SSM_HEAD_DIM
                sc = []
                for k in range(2):
                    col = direction * SSM_HEADS + hd + k
                    seg = cum_n[:, col:col + 1] - cum_t[col:col + 1, :]
                    sc.append((cbm * jnp.where(tri, jnp.exp(seg), 0.0)).astype(BF16))
                xp = xdt[:, l0:l0 + LANES]
                rhs = jnp.concatenate([jnp.where(lane_lo, xp, 0.0), jnp.where(lane_lo, 0.0, xp)],
                                      axis=0).astype(BF16)
                y_parts.append(_dot(jnp.concatenate(sc, axis=1), rhs))
            y_off = _dot(cg, st[:, g * gw:(g + 1) * gw].astype(BF16))
            y_parts[-2] = y_parts[-2] + y_off[:, 0:LANES] * ec[:, g * gw:g * gw + LANES]
            y_parts[-1] = y_parts[-1] + y_off[:, LANES:gw] * ec[:, g * gw + LANES:(g + 1) * gw]
            st_parts.append(_dot_tn(bg, xd[:, g * gw:(g + 1) * gw]))
        y_ref[direction, pl.ds(r0, ck), :] = jnp.concatenate(y_parts, axis=1)
        state_ref[slot] = st * jnp.exp(tot) + jnp.concatenate(st_parts, axis=1)

    for s in range(n_seq):
        for direction in range(2):
            if has_h0:
                state_ref[2 * s + direction] = h0_ref[s, 0, direction].T
            else:
                state_ref[2 * s + direction] = jnp.zeros(state_ref.shape[1:], F32)

    def scan_body(k, carry):
        for s in range(n_seq):
            for direction in range(2):
                c = k if direction == 0 else n_chunk - 1 - k
                chunk_step(pl.multiple_of(s * seq_len + c * ck, ck), direction, 2 * s + direction)
        return carry

    lax.fori_loop(0, n_chunk, scan_body, 0, unroll=2)
    if emit_state:
        for s in range(n_seq):
            for direction in range(2):
                st_ref[s, out_slot, direction] = state_ref[2 * s + direction].T
        for slot in range(st_ref.shape[1]):
            if slot != out_slot:
                st_ref[:, slot] = jnp.zeros((n_seq,) + st_ref.shape[2:], F32)

    z = _dot(h_ref[...], w_in(EV_Z, EV_X))
    y = y_ref[0] + y_ref[1] + xs_ref[...] * (dsk_ref[0:1] + dsk_ref[1:2])
    mix_ref[:, A_WIDTH:] = _rms(y * _silu(z), go_ref[...]).astype(BF16)
    for c0 in range(0, o_ref.shape[1], pw):
        o_ref[:, c0:c0 + pw] = (x_ref[:, c0:c0 + pw]
                                + m[5:6, c0:c0 + pw] * _dot(mix_ref[...], wout_ref[:, c0:c0 + pw]))


def _even_call(x, mod_l, g, p, h0, *, layer, layer_idx, seq_len, n_seq, row0, n_tiles, mod_set, batch,
               n_layers=1, carry=None):
    t, d = x.shape
    rows = seq_len * n_seq
    tile0 = row0 // rows
    has_h0 = h0 is not None
    emit_state = not has_h0
    hp, ns = SSM_INNER, SSM_STATE
    in_specs = [
        pl.BlockSpec((rows, d), lambda i: (tile0 + i, 0)),
        _mod_spec(mod_l, layer, mod_set),
        _gain_spec(*g),
        _layer_window(p["w_in"], layer_idx, EV_DT),
        _resident(p["w_dt"].shape),
        _resident(p["w_out"].shape),
        _resident(p["w_s"].shape),
        _resident(p["b_s"].shape),
        _resident(p["g_v"].shape),
        _resident(p["w_conv"].shape),
        _resident(p["b_conv"].shape),
        _resident(p["dt_bias"].shape),
        _resident(p["a_log"].shape),
        _resident(p["d_skip"].shape),
        _resident(p["g_out"].shape),
        _resident(p["expand"].shape),
    ]
    args = [x, mod_l, g[0], p["w_in"], p["w_dt"], p["w_out"], p["w_s"], p["b_s"], p["g_v"], p["w_conv"],
            p["b_conv"], p["dt_bias"], p["a_log"], p["d_skip"], p["g_out"], p["expand"]]
    out_shape = [jax.ShapeDtypeStruct((t, d), F32)]
    out_specs = [pl.BlockSpec((rows, d), lambda i: (tile0 + i, 0))]
    if has_h0:
        in_specs.append(pl.BlockSpec((n_seq, 1, 2, hp, ns), lambda i: (i, layer_idx, 0, 0, 0)))
        args.append(h0)
    aliases = {0: 0}
    if emit_state:
        out_shape.append(jax.ShapeDtypeStruct((batch, n_layers, 2, hp, ns), F32))
        if carry is None:
            out_specs.append(pl.BlockSpec((n_seq, n_layers, 2, hp, ns), lambda i: (i, 0, 0, 0, 0)))
        else:
            out_specs.append(pl.BlockSpec((n_seq, 1, 2, hp, ns), lambda i: (i, layer_idx, 0, 0, 0)))
            in_specs.append(pl.BlockSpec(memory_space=pl.ANY))
            args.append(carry)
            aliases[len(args) - 1] = 1
    out_slot = layer_idx if carry is None else 0
    scratch = [
        pltpu.VMEM((rows, d), BF16),
        pltpu.VMEM((rows, A_WIDTH + SSM_INNER), BF16),
        pltpu.VMEM((rows, SSM_INNER), F32),
        pltpu.VMEM((rows, SSM_BC), BF16),
        pltpu.VMEM((rows, SSM_BC), BF16),
        pltpu.VMEM((rows, LANES), F32),
        pltpu.VMEM((rows, LANES), F32),
        pltpu.VMEM((2, rows, SSM_INNER), F32),
        pltpu.VMEM((2 * n_seq, SSM_STATE, SSM_INNER), F32),
    ]
    res = pl.pallas_call(
        functools.partial(_even_kernel, seq_len=seq_len, n_seq=n_seq, has_h0=has_h0, emit_state=emit_state,
                          n_carry=len(aliases) - 1, out_slot=out_slot),
        out_shape=out_shape,
        grid=(n_tiles,),
        in_specs=in_specs,
        out_specs=out_specs,
        scratch_shapes=scratch,
        input_output_aliases=aliases,
        compiler_params=pltpu.CompilerParams(
            dimension_semantics=("arbitrary",), vmem_limit_bytes=VMEM_LIMIT),
        name="even_mixer_latent" if has_h0 else "even_mixer_context",
    )(*args)
    return res


def _pack_even(w_in, w_out, w_s, b_s, g_v, w_conv, b_conv, dt_bias, a_log, d_skip, g_out):
    d = w_in.shape[0]
    n_dt = 2 * SSM_HEADS
    w_dt = jnp.concatenate([w_in[:, EV_DT:], jnp.zeros((d, LANES - n_dt), F32)], axis=1)

    def narrow(v):
        return jnp.concatenate([v.reshape(1, n_dt), jnp.zeros((1, LANES - n_dt), F32)], axis=1)

    lane_head = jnp.arange(2 * SSM_INNER) // SSM_HEAD_DIM
    expand = (jnp.arange(LANES)[:, None] == lane_head[None, :]).astype(BF16)
    return {
        "w_dt": w_dt.astype(BF16),
        "w_out": w_out.astype(BF16),
        "w_s": w_s.astype(BF16),
        "b_s": jnp.broadcast_to(b_s[:, :, None], b_s.shape + (LANES,)),
        "g_v": g_v.reshape(1, -1),
        "w_conv": w_conv,
        "b_conv": b_conv.reshape(1, -1),
        "dt_bias": narrow(dt_bias),
        "a_log": narrow(a_log),
        "d_skip": jnp.repeat(d_skip, SSM_HEAD_DIM, axis=1),
        "g_out": g_out.reshape(1, -1),
        "expand": expand,
    }


MLA_Q_RANK = 384
MLA_KV_RANK = 256
CONV_WIDTH = 512
HEAD_W = LANES
ROPE_L0 = MLA_NOPE
OD_CQ = 0
OD_CKV = OD_CQ + MLA_Q_RANK
OD_LAT = OD_CKV + MLA_KV_RANK
CONV_PAD = 16
CONV_RB = 64
ATT_QB = 256


def _odd_kernel(x_ref, mod_ref, g_ref, win_ref, wag_ref, wkr_ref, wuq_ref, wuk_ref, wuv_ref, wout_ref, gcq_ref, gckv_ref,
                wdw_ref, bdw_ref, gln_ref, bln_ref, *rest, seq_len, n_seq, past, n_carry, out_slot):
    rest = list(rest)
    has_cache = past > 0
    if has_cache:
        cckv_ref, ckr_ref, cos_ref, sin_ref = rest[:4]
        o_ref = rest[4]
        rest = rest[5:]
    else:
        rest = rest[n_carry:]
        o_ref, ckv_out_ref, kr_out_ref = rest[:3]
        rest = rest[3:]
    h_ref, cq_ref, ckv_ref, kr_ref, mix_ref, q_ref, k_ref, v_ref, oh_ref, pad_ref, cout_ref = rest

    nh = MLA_HEADS
    hw = nh * MLA_V
    att_qb = min(ATT_QB, seq_len)
    first_head_lanes = lax.broadcasted_iota(jnp.int32, (att_qb, HEAD_W), 1) < MLA_V
    n_keys = past + seq_len
    scale = (MLA_NOPE + MLA_ROPE) ** -0.5
    m = mod_ref[0, 0]
    h_ref[...] = (_rms(x_ref[...], g_ref[0]) * (1.0 + m[4:5]) + m[3:4]).astype(BF16)

    cq_ref[...] = _rms(_dot(h_ref[...], win_ref[0, :, OD_CQ:OD_CKV].astype(BF16)), gcq_ref[...]).astype(BF16)
    ckv = _rms(_dot(h_ref[...], win_ref[0, :, OD_CKV:OD_LAT].astype(BF16)), gckv_ref[...])
    ckv_ref[...] = ckv.astype(BF16)
    kr = _dot(h_ref[...], wkr_ref[:, 0:HEAD_W])
    if has_cache:
        cos = jnp.concatenate([cos_ref[...]] * n_seq, axis=0)
        sin = jnp.concatenate([sin_ref[...]] * n_seq, axis=0)
        kr = kr * cos + _dot(h_ref[...], wkr_ref[:, HEAD_W:2 * HEAD_W]) * sin
    else:
        for s in range(n_seq):
            ckv_out_ref[s, out_slot] = ckv[s * seq_len:(s + 1) * seq_len]
            kr_out_ref[s, out_slot] = kr[s * seq_len:(s + 1) * seq_len]
        for slot in range(ckv_out_ref.shape[1]):
            if slot != out_slot:
                ckv_out_ref[:, slot] = jnp.zeros((n_seq,) + ckv_out_ref.shape[2:], F32)
                kr_out_ref[:, slot] = jnp.zeros((n_seq,) + kr_out_ref.shape[2:], F32)
    kr_ref[...] = kr

    n_pair = nh // 2

    def twice(a):
        return jnp.concatenate([a, a], axis=1)

    def q_body(pr, carry):
        q = _dot(cq_ref[...], wuq_ref[pr])
        if has_cache:
            q = q * twice(cos) + _dot(cq_ref[...], wuq_ref[n_pair + pr]) * twice(sin)
        q_ref[pr] = q.astype(BF16)
        return carry

    lax.fori_loop(0, n_pair, q_body, 0, unroll=4)

    n_qb = seq_len // att_qb
    pw = 2 * LANES
    n_lt = CONV_WIDTH // LANES
    nv = CONV_RB // 8
    zpad = jnp.zeros((n_lt, CONV_PAD, LANES), F32)
    for s in range(n_seq):
        r0 = s * seq_len

        def kv_body(pr, carry, s=s, r0=r0):
            lat = ckv_ref[r0:r0 + seq_len]
            k_ref[pr, past:n_keys] = (_dot(lat, wuk_ref[pr]) + twice(kr_ref[r0:r0 + seq_len])).astype(BF16)
            v_ref[pr, past:n_keys] = _dot(lat, wuv_ref[pr]).astype(BF16)
            if has_cache:
                lat_c = cckv_ref[s, 0].astype(BF16)
                k_ref[pr, 0:past] = (_dot(lat_c, wuk_ref[pr]) + twice(ckr_ref[s, 0])).astype(BF16)
                v_ref[pr, 0:past] = _dot(lat_c, wuv_ref[pr]).astype(BF16)
            return carry

        lax.fori_loop(0, n_pair, kv_body, 0, unroll=4)

        def att_body(it, carry, r0=r0):
            pr = it // n_qb
            q0 = pl.multiple_of(r0 + (it % n_qb) * att_qb, att_qb)
            outs = []
            for l0 in range(0, 2 * HEAD_W, HEAD_W):
                sc = _dot_nt(q_ref[pr, pl.ds(q0, att_qb), l0:l0 + HEAD_W], k_ref[pr, :, l0:l0 + HEAD_W]) * scale
                p = jnp.exp(sc - jnp.max(sc, axis=-1, keepdims=True))
                den = jnp.sum(p, axis=-1, keepdims=True)
                outs.append(_dot(p.astype(BF16), v_ref[pr]) / den)
            oh_ref[pr, pl.ds(q0, att_qb), :] = jnp.where(first_head_lanes, outs[0], outs[1]).astype(BF16)
            return carry

        pad_ref[:, 0:CONV_PAD] = zpad
        pad_ref[:, CONV_PAD + seq_len:2 * CONV_PAD + seq_len] = zpad
        for c0 in range(0, CONV_WIDTH, pw):
            a = _dot(h_ref[r0:r0 + seq_len], wag_ref[:, c0:c0 + pw])
            gt = _dot(h_ref[r0:r0 + seq_len], wag_ref[:, CONV_WIDTH + c0:CONV_WIDTH + c0 + pw])
            glu = a * _sigmoid(gt)
            for l0 in range(0, pw, LANES):
                pad_ref[(c0 + l0) // LANES, CONV_PAD:CONV_PAD + seq_len] = glu[:, l0:l0 + LANES]

        def conv_body(b, carry, r0=r0):
            base = pl.multiple_of(b * CONV_RB, CONV_RB)
            for lt in range(n_lt):
                l0 = lt * LANES
                accs = [jnp.broadcast_to(bdw_ref[:, l0:l0 + LANES], (8, LANES))] * nv
                for k in range(CONV_K):
                    wk = jnp.broadcast_to(wdw_ref[k:k + 1, l0:l0 + LANES], (8, LANES))
                    off = k + CONV_PAD - CONV_K // 2
                    accs = [accs[v] + wk * pad_ref[lt, pl.ds(base + v + off, 8, stride=nv), :] for v in range(nv)]
                for v in range(nv):
                    cout_ref[lt, pl.ds(base + v, 8, stride=nv), :] = accs[v]
            acc = jnp.concatenate([cout_ref[lt, pl.ds(base, CONV_RB), :] for lt in range(n_lt)], axis=1)
            dc = acc - jnp.mean(acc, axis=-1, keepdims=True)
            var = jnp.mean(dc * dc, axis=-1, keepdims=True)
            dn = dc * lax.rsqrt(var + EPS) * gln_ref[...] + bln_ref[...]
            mix_ref[pl.ds(r0 + base, CONV_RB), hw:] = _silu(dn).astype(BF16)
            return carry

        lax.fori_loop(0, n_pair * n_qb, att_body, 0, unroll=4)
        lax.fori_loop(0, seq_len // CONV_RB, conv_body, 0, unroll=4)

    for pr in range(n_pair):
        mix_ref[:, pr * HEAD_W:(pr + 1) * HEAD_W] = oh_ref[pr]

    for c0 in range(0, o_ref.shape[1], pw):
        o_ref[:, c0:c0 + pw] = (x_ref[:, c0:c0 + pw]
                                + m[5:6, c0:c0 + pw] * _dot(mix_ref[...], wout_ref[:, c0:c0 + pw]))


def _odd_call(x, mod_l, g, p, cache, *, layer, layer_idx, seq_len, n_seq, row0, n_tiles, mod_set, batch,
              n_layers=1, carry=None):
    t, d = x.shape
    rows = seq_len * n_seq
    tile0 = row0 // rows
    has_cache = cache is not None
    past = cache[0].shape[2] if has_cache else 0
    names = ["w_in", "w_ag", "w_kr", "w_uq", "w_uk", "w_uv", "w_out", "g_cq", "g_ckv", "w_dw", "b_dw", "g_ln", "b_ln"]
    in_specs = [
        pl.BlockSpec((rows, d), lambda i: (tile0 + i, 0)),
        _mod_spec(mod_l, layer, mod_set),
        _gain_spec(*g),
    ] + [_layer_window(p[k], layer_idx, OD_LAT) if k == "w_in" else _resident(p[k].shape) for k in names]
    args = [x, mod_l, g[0]] + [p[k] for k in names]
    out_shape = [jax.ShapeDtypeStruct((t, d), F32)]
    out_specs = [pl.BlockSpec((rows, d), lambda i: (tile0 + i, 0))]
    aliases = {0: 0}
    if has_cache:
        cckv, ckr, cos, sin = cache
        in_specs += [
            pl.BlockSpec((n_seq, 1, past, MLA_KV_RANK), lambda i: (i, layer_idx, 0, 0)),
            pl.BlockSpec((n_seq, 1, past, HEAD_W), lambda i: (i, layer_idx, 0, 0)),
            _resident(cos.shape),
            _resident(sin.shape),
        ]
        args += [cckv, ckr, cos, sin]
    else:
        out_shape += [jax.ShapeDtypeStruct((batch, n_layers, seq_len, MLA_KV_RANK), F32),
                      jax.ShapeDtypeStruct((batch, n_layers, seq_len, HEAD_W), F32)]
        if carry is None:
            out_specs += [pl.BlockSpec((n_seq, n_layers, seq_len, MLA_KV_RANK), lambda i: (i, 0, 0, 0)),
                          pl.BlockSpec((n_seq, n_layers, seq_len, HEAD_W), lambda i: (i, 0, 0, 0))]
        else:
            out_specs += [pl.BlockSpec((n_seq, 1, seq_len, MLA_KV_RANK), lambda i: (i, layer_idx, 0, 0)),
                          pl.BlockSpec((n_seq, 1, seq_len, HEAD_W), lambda i: (i, layer_idx, 0, 0))]
            for k, arr in enumerate(carry):
                in_specs.append(pl.BlockSpec(memory_space=pl.ANY))
                args.append(arr)
                aliases[len(args) - 1] = 1 + k
    n_keys = past + seq_len
    scratch = [
        pltpu.VMEM((rows, d), BF16),
        pltpu.VMEM((rows, MLA_Q_RANK), BF16),
        pltpu.VMEM((rows, MLA_KV_RANK), BF16),
        pltpu.VMEM((rows, HEAD_W), F32),
        pltpu.VMEM((rows, MLA_HEADS * MLA_V + CONV_WIDTH), BF16),
        pltpu.VMEM((MLA_HEADS // 2, rows, 2 * HEAD_W), BF16),
        pltpu.VMEM((MLA_HEADS // 2, n_keys, 2 * HEAD_W), BF16),
        pltpu.VMEM((MLA_HEADS // 2, n_keys, 2 * MLA_V), BF16),
        pltpu.VMEM((MLA_HEADS // 2, rows, 2 * MLA_V), BF16),
        pltpu.VMEM((CONV_WIDTH // LANES, seq_len + 2 * CONV_PAD, LANES), F32),
        pltpu.VMEM((CONV_WIDTH // LANES, seq_len, LANES), F32),
    ]
    return pl.pallas_call(
        functools.partial(_odd_kernel, seq_len=seq_len, n_seq=n_seq, past=past, n_carry=len(aliases) - 1,
                          out_slot=layer_idx if (not has_cache and carry is None) else 0),
        out_shape=out_shape,
        grid=(n_tiles,),
        in_specs=in_specs,
        out_specs=out_specs,
        scratch_shapes=scratch,
        input_output_aliases=aliases,
        compiler_params=pltpu.CompilerParams(
            dimension_semantics=("arbitrary",), vmem_limit_bytes=VMEM_LIMIT),
        name="odd_mixer_latent" if has_cache else "odd_mixer_context",
    )(*args)


def _head_tiles(w, per_head, start, width, lane0):
    k = w.shape[0]
    blk = w.reshape(k, MLA_HEADS, per_head)[:, :, start:start + width]
    out = jnp.zeros((k, MLA_HEADS, HEAD_W), w.dtype).at[:, :, lane0:lane0 + width].set(blk)
    return out.reshape(k, MLA_HEADS // 2, 2 * HEAD_W).transpose(1, 0, 2)


def _rope_swap(w):
    q = MLA_ROPE // 4
    idx = jnp.concatenate([jnp.arange(q, 2 * q), jnp.arange(0, q), jnp.arange(3 * q, 4 * q), jnp.arange(2 * q, 3 * q)])
    return w[..., idx]


def _pack_odd(w_in, w_out, g_cq, w_uq, g_ckv, w_ukv, w_dw, b_dw, g_ln, b_ln):
    d = w_in.shape[0]
    o1 = MLA_Q_RANK
    o2 = o1 + MLA_KV_RANK
    o3 = o2 + MLA_ROPE
    w_kr = w_in[:, o2:o3]

    def kr_tile(w):
        return jnp.zeros((d, HEAD_W), F32).at[:, ROPE_L0:ROPE_L0 + MLA_ROPE].set(w)

    w_kr_p = jnp.concatenate([kr_tile(w_kr), kr_tile(_rope_swap(w_kr))], axis=1)
    qh = MLA_NOPE + MLA_ROPE
    rank = w_uq.shape[0]
    w_q3 = w_uq.reshape(rank, MLA_HEADS, qh)
    w_q_sw = jnp.concatenate([jnp.zeros((rank, MLA_HEADS, MLA_NOPE), F32), _rope_swap(w_q3[:, :, MLA_NOPE:])],
                             axis=2).reshape(rank, MLA_HEADS * qh)
    w_uq_p = jnp.concatenate([_head_tiles(w_uq, qh, 0, qh, 0),
                              _head_tiles(w_q_sw, qh, MLA_NOPE, MLA_ROPE, ROPE_L0)], axis=0)
    kvh = MLA_NOPE + MLA_V
    w_uk_p = _head_tiles(w_ukv, kvh, 0, MLA_NOPE, 0)
    rank_kv = w_ukv.shape[0]
    w_uv_p = (w_ukv.reshape(rank_kv, MLA_HEADS, kvh)[:, :, MLA_NOPE:]
              .reshape(rank_kv, MLA_HEADS // 2, 2 * MLA_V).transpose(1, 0, 2))
    return {
        "w_ag": w_in[:, o3:].astype(BF16),
        "w_kr": w_kr_p.astype(BF16),
        "w_uq": w_uq_p.astype(BF16),
        "w_uk": w_uk_p.astype(BF16),
        "w_uv": w_uv_p.astype(BF16),
        "w_out": w_out.astype(BF16),
        "g_cq": g_cq.reshape(1, -1),
        "g_ckv": g_ckv.reshape(1, -1),
        "w_dw": w_dw,
        "b_dw": b_dw.reshape(1, -1),
        "g_ln": g_ln.reshape(1, -1),
        "b_ln": b_ln.reshape(1, -1),
    }


def _rope_tables(length):
    n_freq = MLA_ROPE // 4
    pos = jnp.arange(length)
    row = (pos // GRID_W).astype(F32)
    col = (pos % GRID_W).astype(F32)
    freqs = ROPE_THETA ** (-jnp.arange(n_freq, dtype=F32) / n_freq)
    ar = row[:, None] * freqs
    ac = col[:, None] * freqs
    cos = jnp.concatenate([jnp.cos(ar), jnp.cos(ar), jnp.cos(ac), jnp.cos(ac)], axis=1)
    sin = jnp.concatenate([-jnp.sin(ar), jnp.sin(ar), -jnp.sin(ac), jnp.sin(ac)], axis=1)
    tail = HEAD_W - ROPE_L0 - MLA_ROPE
    cos_t = jnp.concatenate([jnp.ones((length, ROPE_L0), F32), cos, jnp.zeros((length, tail), F32)], axis=1)
    sin_t = jnp.concatenate([jnp.zeros((length, ROPE_L0), F32), sin, jnp.zeros((length, tail), F32)], axis=1)
    return cos_t, sin_t


def kernel(x_prompt, x_sample, state_ssd, cache_mla_ckv, cache_mla_krope, c, c_ctx, w_mod, b_mod, g_norm, w_ff_gu, w_ff_down, w_in_even, w_out_even, w_spatial, b_spatial, g_gmlp_v, w_conv_ssm, b_conv_ssm, dt_bias, a_log, d_skip, g_ssm_out, w_in_odd, w_out_odd, g_cq, w_uq, g_ckv, w_ukv, w_dwconv, b_dwconv, g_conv_ln, b_conv_ln, g_final):
    bp, seq, d = x_prompt.shape
    db, dec_seq, _ = x_sample.shape
    n_ctx = bp * seq
    cond_t = jnp.concatenate([c_ctx[None], c], axis=0).T
    mod = _mod_call(cond_t, w_mod, b_mod).reshape(w_mod.shape[0], 1 + db, N_MOD, d)
    depth = w_mod.shape[0]
    ffn = functools.partial(_ffn_call, tm=512, n_ctx_rows=n_ctx, dec_seq=dec_seq)
    ctx_seqs = 4
    ctx_tile = dict(seq_len=seq, n_seq=ctx_seqs, row0=0, n_tiles=bp // ctx_seqs, mod_set=lambda i: 0, batch=bp)
    lat_tile = dict(seq_len=dec_seq, n_seq=1, row0=n_ctx, n_tiles=db, mod_set=lambda i: 1 + i, batch=bp)
    h0 = state_ssd.reshape(db, state_ssd.shape[1], 2, SSM_INNER, SSM_STATE)
    ckr_tile = jnp.pad(cache_mla_krope, ((0, 0), (0, 0), (0, 0), (ROPE_L0, HEAD_W - ROPE_L0 - MLA_ROPE)))
    cos_t, sin_t = _rope_tables(dec_seq)
    n_even, n_odd = (depth + 1) // 2, depth // 2
    new_state = new_cache = None
    xs = [x_prompt.reshape(n_ctx, d), x_sample.reshape(db * dec_seq, d)]
    n_gain = g_norm.shape[1]
    gains = g_norm.reshape(depth * n_gain, 1, d)
    for l in range(depth):
        i = l // 2
        last = l == depth - 1
        x = ffn(xs, mod, (gains, l * n_gain), w_ff_gu, w_ff_down, None, layer=l, which=0, mrow=0)
        g_mix = (gains, l * n_gain + 1)
        if l % 2 == 0:
            pe = _pack_even(w_in_even[i], w_out_even[i], w_spatial[i], b_spatial[i], g_gmlp_v[i], w_conv_ssm[i],
                            b_conv_ssm[i], dt_bias[i], a_log[i], d_skip[i], g_ssm_out[i])
            pe["w_in"] = w_in_even
            x, new_state = _even_call(x, mod, g_mix, pe, None, layer=l, layer_idx=i, n_layers=n_even,
                                      carry=new_state, **ctx_tile)
            (x,) = _even_call(x, mod, g_mix, pe, h0, layer=l, layer_idx=i, **lat_tile)
        else:
            po = _pack_odd(w_in_odd[i], w_out_odd[i], g_cq[i], w_uq[i], g_ckv[i], w_ukv[i], w_dwconv[i],
                           b_dwconv[i], g_conv_ln[i], b_conv_ln[i])
            po["w_in"] = w_in_odd
            x, *new_cache = _odd_call(x, mod, g_mix, po, None, layer=l, layer_idx=i, n_layers=n_odd,
                                      carry=new_cache, **ctx_tile)
            (x,) = _odd_call(x, mod, g_mix, po, (cache_mla_ckv, ckr_tile, cos_t, sin_t),
                             layer=l, layer_idx=i, **lat_tile)
        res = ffn([x], mod, (gains, l * n_gain + 2), w_ff_gu, w_ff_down, g_final if last else None,
                  layer=l, which=1, mrow=6)
        xs = [res]
    y_ctx, y_lat = res
    new_state = new_state.reshape(bp, n_even, 2, SSM_HEADS, SSM_HEAD_DIM, SSM_STATE)
    new_kr = new_cache[1][..., ROPE_L0:ROPE_L0 + MLA_ROPE]
    return (y_ctx.reshape(bp, seq, d), y_lat.reshape(db, dec_seq, d), new_state, new_cache[0], new_kr)
```

```python
import functools

import jax
import jax.numpy as jnp
from jax import lax
from jax.experimental import pallas as pl
from jax.experimental.pallas import tpu as pltpu

F32 = jnp.float32
BF16 = jnp.bfloat16

EPS = 1e-6
N_MOD = 9
LANES = 128
A_GROUPS = 4
A_CHUNK = 128
SSM_HEADS = 8
SSM_HEAD_DIM = 64
SSM_GROUPS = 2
SSM_STATE = 128
SSM_CHUNK = 128
MLA_HEADS = 8
MLA_NOPE = 64
MLA_ROPE = 32
MLA_V = 64
CONV_K = 31
GRID_W = 64
ROPE_THETA = 10000.0

VMEM_LIMIT = 60 * 1024 * 1024


def _dot(a, b):
    return jnp.dot(a, b, preferred_element_type=F32)


def _dot_nt(a, b):
    return lax.dot_general(a, b, (((1,), (1,)), ((), ())), preferred_element_type=F32)


def _dot_tn(a, b):
    return lax.dot_general(a, b, (((0,), (0,)), ((), ())), preferred_element_type=F32)


def _rms(x, g):
    return x * lax.rsqrt(jnp.mean(x * x, axis=-1, keepdims=True) + EPS) * g


def _sigmoid(x):
    return 1.0 / (1.0 + jnp.exp(-x))


def _silu(x):
    return x * _sigmoid(x)


def _mod_index(tile_rows, n_ctx_rows, dec_seq):
    def index(i):
        return jnp.maximum((i * tile_rows - n_ctx_rows) // dec_seq + 1, 0)
    return index


def _mod_spec(mod_all, layer, set_of_tile):
    return pl.BlockSpec((1, 1) + mod_all.shape[2:], lambda i: (layer, set_of_tile(i), 0, 0))


def _gain_spec(g_all, idx):
    return pl.BlockSpec((1,) + g_all.shape[1:], lambda i: (idx, 0, 0), pipeline_mode=pl.Buffered(1))


def _resident(shape):
    nd = len(shape)
    return pl.BlockSpec(shape, lambda i: (0,) * nd, pipeline_mode=pl.Buffered(1))


def _mod_kernel(ct_ref, w_ref, b_ref, o_ref, *, n_sets):
    w = w_ref[0]
    s = _silu(ct_ref[...])
    rows = [jnp.sum(w * s[:, r:r + 1], axis=0, keepdims=True) for r in range(n_sets)]
    o_ref[0] = jnp.concatenate(rows, axis=0) + b_ref[0]


def _mod_call(cond_t, w_mod, b_mod):
    depth, d, n = w_mod.shape
    n_sets = cond_t.shape[1]
    tn = n // 4
    return pl.pallas_call(
        functools.partial(_mod_kernel, n_sets=n_sets),
        out_shape=jax.ShapeDtypeStruct((depth, n_sets, n), F32),
        grid=(depth, n // tn),
        in_specs=[
            pl.BlockSpec((d, n_sets), lambda l, j: (0, 0)),
            pl.BlockSpec((1, d, tn), lambda l, j: (l, 0, j)),
            pl.BlockSpec((1, 1, tn), lambda l, j: (l, 0, j)),
        ],
        out_specs=pl.BlockSpec((1, n_sets, tn), lambda l, j: (l, 0, j)),
        compiler_params=pltpu.CompilerParams(
            dimension_semantics=("arbitrary", "arbitrary"), vmem_limit_bytes=VMEM_LIMIT),
        name="adaln_mod",
    )(cond_t, w_mod, b_mod.reshape(depth, 1, n))


FFN_CHUNK = 2 * LANES


def _ffn_kernel(*refs, mrow, d_ff, ctx_tiles, first, final, layer, which):
    refs = list(refs)
    n_x = 2 if first else 1
    x_refs, (mod_ref, g_ref, wgu_hbm, wd_hbm), refs = refs[:n_x], refs[n_x:n_x + 4], refs[n_x + 4:]
    wgu_ref, wd_ref, sem = refs[-3:]
    i = pl.program_id(0)

    def weight_copies(c0):
        cols, rows = pl.ds(c0, FFN_CHUNK), pl.ds(c0, FFN_CHUNK)
        ucols = pl.ds(d_ff + c0, FFN_CHUNK)
        k = c0 // FFN_CHUNK
        return (pltpu.make_async_copy(wgu_hbm.at[layer, which, :, cols], wgu_ref.at[:, cols], sem.at[0, k]),
                pltpu.make_async_copy(wgu_hbm.at[layer, which, :, ucols], wgu_ref.at[:, ucols], sem.at[1, k]),
                pltpu.make_async_copy(wd_hbm.at[layer, which, rows, :], wd_ref.at[rows, :], sem.at[2, k]))

    def run(first_step):
        if first and first_step:
            x = x_refs[0][...]
        elif first:
            x = jnp.where(i < ctx_tiles, x_refs[0][...], x_refs[1][...])
        else:
            x = x_refs[0][...]
        m = mod_ref[0, 0]
        shift, scale, gate = m[mrow:mrow + 1], m[mrow + 1:mrow + 2], m[mrow + 2:mrow + 3]
        h = (_rms(x, g_ref[0]) * (1.0 + scale) + shift).astype(BF16)
        acc = jnp.zeros(x.shape, F32)
        for c0 in range(0, d_ff, FFN_CHUNK):
            if first_step:
                for cp in weight_copies(c0):
                    cp.wait()
            g = _dot(h, wgu_ref[:, c0:c0 + FFN_CHUNK].astype(BF16))
            u = _dot(h, wgu_ref[:, d_ff + c0:d_ff + c0 + FFN_CHUNK].astype(BF16))
            a = (_silu(g) * u).astype(BF16)
            acc = acc + _dot(a, wd_ref[c0:c0 + FFN_CHUNK, :].astype(BF16))
        out = x + 0.5 * gate * acc
        if not final:
            refs[0][...] = out
            return
        gf_ref, yc_ref, yl_ref = refs[:3]
        y = _rms(out, gf_ref[...])
        if first_step:
            yc_ref[...] = y
            return

        @pl.when(i < ctx_tiles)
        def _():
            yc_ref[...] = y

        @pl.when(i >= ctx_tiles)
        def _():
            yl_ref[...] = y

    @pl.when(i == 0)
    def _():
        for c0 in range(0, d_ff, FFN_CHUNK):
            for cp in weight_copies(c0):
                cp.start()
        run(True)

    @pl.when(i > 0)
    def _():
        run(False)


def _ffn_call(xs, mod_l, g, w_gu, w_d, g_final, *, layer, which, mrow, tm, n_ctx_rows, dec_seq):
    first = len(xs) == 2
    final = g_final is not None
    d = xs[0].shape[1]
    t = sum(a.shape[0] for a in xs)
    d_ff = w_d.shape[2]
    ctx_tiles = n_ctx_rows // tm

    def ctx_map(i):
        return (jnp.minimum(i, ctx_tiles - 1), 0)

    def lat_map(i):
        return (jnp.maximum(i - ctx_tiles, 0), 0)

    x_specs = ([pl.BlockSpec((tm, d), ctx_map), pl.BlockSpec((tm, d), lat_map)] if first
               else [pl.BlockSpec((tm, d), lambda i: (i, 0))])
    in_specs = x_specs + [
        _mod_spec(mod_l, layer, _mod_index(tm, n_ctx_rows, dec_seq)),
        _gain_spec(*g),
        pl.BlockSpec(memory_space=pl.ANY),
        pl.BlockSpec(memory_space=pl.ANY),
    ]
    scratch = [pltpu.VMEM(w_gu.shape[2:], F32), pltpu.VMEM(w_d.shape[2:], F32),
               pltpu.SemaphoreType.DMA((3, d_ff // FFN_CHUNK))]
    args = list(xs) + [mod_l, g[0], w_gu, w_d]
    if final:
        in_specs.append(_resident((1, d)))
        args.append(g_final.reshape(1, d))
        out_shape = (jax.ShapeDtypeStruct((n_ctx_rows, d), F32), jax.ShapeDtypeStruct((t - n_ctx_rows, d), F32))
        out_specs = (pl.BlockSpec((tm, d), ctx_map), pl.BlockSpec((tm, d), lat_map))
    else:
        out_shape = jax.ShapeDtypeStruct((t, d), F32)
        out_specs = pl.BlockSpec((tm, d), lambda i: (i, 0))
    return pl.pallas_call(
        functools.partial(_ffn_kernel, mrow=mrow, d_ff=d_ff, ctx_tiles=ctx_tiles, first=first, final=final,
                          layer=layer, which=which),
        out_shape=out_shape,
        grid=(t // tm,),
        in_specs=in_specs,
        out_specs=out_specs,
        scratch_shapes=scratch,
        compiler_params=pltpu.CompilerParams(
            dimension_semantics=("arbitrary",), vmem_limit_bytes=VMEM_LIMIT),
        name="ffn_final" if final else ("ffn_first" if first else "ffn"),
    )(*args)


A_WIDTH = A_GROUPS * LANES
SSM_INNER = SSM_HEADS * SSM_HEAD_DIM
SSM_BC = SSM_GROUPS * SSM_STATE
EV_UV = 0
EV_Z = EV_UV + 2 * A_WIDTH
EV_X = EV_Z + SSM_INNER
EV_B = EV_X + SSM_INNER
EV_C = EV_B + SSM_BC
EV_DT = EV_C + SSM_BC


def _gelu_tanh(x):
    c, a = 0.7978845608028654, 0.044715
    hx = 0.5 * x
    return hx + hx * jnp.tanh(x * (c + (c * a) * (x * x)))


def _softplus(x):
    return jnp.maximum(x, 0.0) + jnp.log1p(jnp.exp(-jnp.abs(x)))


def _split3(v):
    hi = v.astype(BF16)
    r = v - hi.astype(F32)
    mid = r.astype(BF16)
    lo = (r - mid.astype(F32)).astype(BF16)
    return hi, mid, lo


def _expand(parts, e):
    n = parts[0].shape[0]
    y = _dot(jnp.concatenate(parts, axis=0), e)
    out = y[0:n]
    for k in range(1, len(parts)):
        out = out + y[k * n:(k + 1) * n]
    return out


def _even_kernel(x_ref, mod_ref, g_ref, win_ref, wdt_ref, wout_ref, ws_ref, bs_ref, gv_ref, wc_ref, bc_ref,
                 dtb_ref, alog_ref, dsk_ref, go_ref, e_ref, *rest, seq_len, n_seq, has_h0, emit_state,
                 n_carry, out_slot):
    rest = list(rest)
    h0_ref = rest.pop(0) if has_h0 else None
    rest = rest[n_carry:]
    o_ref = rest.pop(0)
    st_ref = rest.pop(0) if emit_state else None
    h_ref, mix_ref, xs_ref, b_ref, c_ref, dt_ref, cum_ref, y_ref, state_ref = rest

    rows = seq_len * n_seq
    ck = SSM_CHUNK
    n_chunk = seq_len // ck
    m = mod_ref[0, 0]
    h_ref[...] = (_rms(x_ref[...], g_ref[0]) * (1.0 + m[4:5]) + m[3:4]).astype(BF16)

    vb = _rms(_gelu_tanh(_dot(h_ref[...], win_ref[:, EV_UV + A_WIDTH:EV_Z])), gv_ref[...]).astype(BF16)
    for g in range(A_GROUPS):
        l0 = g * LANES
        u = _gelu_tanh(_dot(h_ref[...], win_ref[:, EV_UV + l0:EV_UV + l0 + LANES]))
        for cc in range(rows // A_CHUNK):
            r0 = cc * A_CHUNK
            s = _dot(ws_ref[g], vb[r0:r0 + A_CHUNK, l0:l0 + LANES]) + bs_ref[g]
            mix_ref[r0:r0 + A_CHUNK, l0:l0 + LANES] = (u[r0:r0 + A_CHUNK] * s).astype(BF16)

    pw = 2 * LANES
    rin = lax.broadcasted_iota(jnp.int32, (rows, pw), 0) % seq_len
    for c0 in range(0, EV_DT - EV_X, pw):
        xbc = _dot(h_ref[...], win_ref[:, EV_X + c0:EV_X + c0 + pw])
        wc = wc_ref[:, c0:c0 + pw]
        prev = jnp.where(rin >= 1, pltpu.roll(xbc, 1, 0), 0.0)
        nxt = jnp.where(rin < seq_len - 1, pltpu.roll(xbc, rows - 1, 0), 0.0)
        xbc = _silu(prev * wc[0:1] + xbc * wc[1:2] + nxt * wc[2:3] + bc_ref[:, c0:c0 + pw])
        if c0 < SSM_INNER:
            xs_ref[:, c0:c0 + pw] = xbc
        elif c0 < SSM_INNER + SSM_BC:
            b_ref[:, c0 - SSM_INNER:c0 - SSM_INNER + pw] = xbc.astype(BF16)
        else:
            c_ref[:, c0 - SSM_INNER - SSM_BC:c0 - SSM_INNER - SSM_BC + pw] = xbc.astype(BF16)

    dt = _softplus(_dot(h_ref[...], wdt_ref[...]) + dtb_ref[...])
    lane = lax.broadcasted_iota(jnp.int32, dt.shape, 1)
    rck = lax.broadcasted_iota(jnp.int32, dt.shape, 0) % ck
    da = jnp.where(lane < 2 * SSM_HEADS, dt * -jnp.exp(alog_ref[...]), 0.0)
    cf = da
    cb = da
    sh = 1
    while sh < ck:
        cf = cf + jnp.where(rck >= sh, pltpu.roll(cf, sh, 0), 0.0)
        cb = cb + jnp.where(rck < ck - sh, pltpu.roll(cb, rows - sh, 0), 0.0)
        sh *= 2
    dt_ref[...] = dt
    cum_ref[...] = jnp.where(lane < SSM_HEADS, cf, cb)

    ii = lax.broadcasted_iota(jnp.int32, (ck, ck), 0)
    jj = lax.broadcasted_iota(jnp.int32, (ck, ck), 1)
    lane_lo = lax.broadcasted_iota(jnp.int32, (ck, LANES), 1) < SSM_HEAD_DIM
    gw = SSM_INNER // SSM_GROUPS
    hpg = SSM_HEADS // SSM_GROUPS

    def chunk_step(r0, direction, slot):
        e = e_ref[:, direction * SSM_INNER:(direction + 1) * SSM_INNER]
        cum_n = cum_ref[pl.ds(r0, ck), :]
        cum_t = cum_n.T
        dt_n = dt_ref[pl.ds(r0, ck), :]
        dt_e = jnp.concatenate(
            [jnp.where(lane_lo,
                       jnp.broadcast_to(dt_n[:, direction * SSM_HEADS + hd:direction * SSM_HEADS + hd + 1], (ck, LANES)),
                       jnp.broadcast_to(dt_n[:, direction * SSM_HEADS + hd + 1:direction * SSM_HEADS + hd + 2], (ck, LANES)))
             for hd in range(0, SSM_HEADS, 2)], axis=1)
        cum_e = _expand(_split3(cum_n), e)
        x_c = xs_ref[pl.ds(r0, ck), :]
        xdt = x_c * dt_e
        if direction == 0:
            tot = cum_e[ck - 1:ck, :]
            tri = ii >= jj
        else:
            tot = cum_e[0:1, :]
            tri = ii <= jj
        ec = jnp.exp(cum_e)
        xd = (xdt * jnp.exp(tot - cum_e)).astype(BF16)
        bm = b_ref[pl.ds(r0, ck), :]
        cm = c_ref[pl.ds(r0, ck), :]
        st = state_ref[slot]
        y_parts = []
        st_parts = []
        for g in range(SSM_GROUPS):
            bg = bm[:, g * SSM_STATE:(g + 1) * SSM_STATE]
            cg = cm[:, g * SSM_STATE:(g + 1) * SSM_STATE]
            cbm = _dot_nt(cg, bg)
            for pr in range(hpg // 2):
                hd = g * hpg + 2 * pr
                l0 = hd * SSM_HEAD_DIM
                sc = []
                for k in range(2):
                    col = direction * SSM_HEADS + hd + k
                    seg = cum_n[:, col:col + 1] - cum_t[col:col + 1, :]
                    sc.append((cbm * jnp.where(tri, jnp.exp(seg), 0.0)).astype(BF16))
                xp = xdt[:, l0:l0 + LANES]
                rhs = jnp.concatenate([jnp.where(lane_lo, xp, 0.0), jnp.where(lane_lo, 0.0, xp)],
                                      axis=0).astype(BF16)
                y_parts.append(_dot(jnp.concatenate(sc, axis=1), rhs))
            y_off = _dot(cg, st[:, g * gw:(g + 1) * gw].astype(BF16))
            y_parts[-2] = y_parts[-2] + y_off[:, 0:LANES] * ec[:, g * gw:g * gw + LANES]
            y_parts[-1] = y_parts[-1] + y_off[:, LANES:gw] * ec[:, g * gw + LANES:(g + 1) * gw]
            st_parts.append(_dot_tn(bg, xd[:, g * gw:(g + 1) * gw]))
        y_ref[direction, pl.ds(r0, ck), :] = jnp.concatenate(y_parts, axis=1)
        state_ref[slot] = st * jnp.exp(tot) + jnp.concatenate(st_parts, axis=1)

    for s in range(n_seq):
        for direction in range(2):
            if has_h0:
                state_ref[2 * s + direction] = h0_ref[s, 0, direction].T
            else:
                state_ref[2 * s + direction] = jnp.zeros(state_ref.shape[1:], F32)

    def scan_body(k, carry):
        for s in range(n_seq):
            for direction in range(2):
                c = k if direction == 0 else n_chunk - 1 - k
                chunk_step(pl.multiple_of(s * seq_len + c * ck, ck), direction, 2 * s + direction)
        return carry

    lax.fori_loop(0, n_chunk, scan_body, 0, unroll=2)
    if emit_state:
        for s in range(n_seq):
            for direction in range(2):
                st_ref[s, out_slot, direction] = state_ref[2 * s + direction].T
        for slot in range(st_ref.shape[1]):
            if slot != out_slot:
                st_ref[:, slot] = jnp.zeros((n_seq,) + st_ref.shape[2:], F32)

    z = _dot(h_ref[...], win_ref[:, EV_Z:EV_X])
    y = y_ref[0] + y_ref[1] + xs_ref[...] * (dsk_ref[0:1] + dsk_ref[1:2])
    mix_ref[:, A_WIDTH:] = _rms(y * _silu(z), go_ref[...]).astype(BF16)
    for c0 in range(0, o_ref.shape[1], pw):
        o_ref[:, c0:c0 + pw] = (x_ref[:, c0:c0 + pw]
                                + m[5:6, c0:c0 + pw] * _dot(mix_ref[...], wout_ref[:, c0:c0 + pw]))


def _even_call(x, mod_l, g, p, h0, *, layer, layer_idx, seq_len, n_seq, row0, n_tiles, mod_set, batch,
               n_layers=1, carry=None):
    t, d = x.shape
    rows = seq_len * n_seq
    tile0 = row0 // rows
    has_h0 = h0 is not None
    emit_state = not has_h0
    hp, ns = SSM_INNER, SSM_STATE
    in_specs = [
        pl.BlockSpec((rows, d), lambda i: (tile0 + i, 0)),
        _mod_spec(mod_l, layer, mod_set),
        _gain_spec(*g),
        _resident(p["w_in"].shape),
        _resident(p["w_dt"].shape),
        _resident(p["w_out"].shape),
        _resident(p["w_s"].shape),
        _resident(p["b_s"].shape),
        _resident(p["g_v"].shape),
        _resident(p["w_conv"].shape),
        _resident(p["b_conv"].shape),
        _resident(p["dt_bias"].shape),
        _resident(p["a_log"].shape),
        _resident(p["d_skip"].shape),
        _resident(p["g_out"].shape),
        _resident(p["expand"].shape),
    ]
    args = [x, mod_l, g[0], p["w_in"], p["w_dt"], p["w_out"], p["w_s"], p["b_s"], p["g_v"], p["w_conv"],
            p["b_conv"], p["dt_bias"], p["a_log"], p["d_skip"], p["g_out"], p["expand"]]
    out_shape = [jax.ShapeDtypeStruct((t, d), F32)]
    out_specs = [pl.BlockSpec((rows, d), lambda i: (tile0 + i, 0))]
    if has_h0:
        in_specs.append(pl.BlockSpec((n_seq, 1, 2, hp, ns), lambda i: (i, layer_idx, 0, 0, 0)))
        args.append(h0)
    aliases = {0: 0}
    if emit_state:
        out_shape.append(jax.ShapeDtypeStruct((batch, n_layers, 2, hp, ns), F32))
        if carry is None:
            out_specs.append(pl.BlockSpec((n_seq, n_layers, 2, hp, ns), lambda i: (i, 0, 0, 0, 0)))
        else:
            out_specs.append(pl.BlockSpec((n_seq, 1, 2, hp, ns), lambda i: (i, layer_idx, 0, 0, 0)))
            in_specs.append(pl.BlockSpec(memory_space=pl.ANY))
            args.append(carry)
            aliases[len(args) - 1] = 1
    out_slot = layer_idx if carry is None else 0
    scratch = [
        pltpu.VMEM((rows, d), BF16),
        pltpu.VMEM((rows, A_WIDTH + SSM_INNER), BF16),
        pltpu.VMEM((rows, SSM_INNER), F32),
        pltpu.VMEM((rows, SSM_BC), BF16),
        pltpu.VMEM((rows, SSM_BC), BF16),
        pltpu.VMEM((rows, LANES), F32),
        pltpu.VMEM((rows, LANES), F32),
        pltpu.VMEM((2, rows, SSM_INNER), F32),
        pltpu.VMEM((2 * n_seq, SSM_STATE, SSM_INNER), F32),
    ]
    res = pl.pallas_call(
        functools.partial(_even_kernel, seq_len=seq_len, n_seq=n_seq, has_h0=has_h0, emit_state=emit_state,
                          n_carry=len(aliases) - 1, out_slot=out_slot),
        out_shape=out_shape,
        grid=(n_tiles,),
        in_specs=in_specs,
        out_specs=out_specs,
        scratch_shapes=scratch,
        input_output_aliases=aliases,
        compiler_params=pltpu.CompilerParams(
            dimension_semantics=("arbitrary",), vmem_limit_bytes=VMEM_LIMIT),
        name="even_mixer_latent" if has_h0 else "even_mixer_context",
    )(*args)
    return res


def _pack_even(w_in, w_out, w_s, b_s, g_v, w_conv, b_conv, dt_bias, a_log, d_skip, g_out):
    d = w_in.shape[0]
    n_dt = 2 * SSM_HEADS
    w_dt = jnp.concatenate([w_in[:, EV_DT:], jnp.zeros((d, LANES - n_dt), F32)], axis=1)

    def narrow(v):
        return jnp.concatenate([v.reshape(1, n_dt), jnp.zeros((1, LANES - n_dt), F32)], axis=1)

    lane_head = jnp.arange(2 * SSM_INNER) // SSM_HEAD_DIM
    expand = (jnp.arange(LANES)[:, None] == lane_head[None, :]).astype(BF16)
    return {
        "w_in": w_in[:, :EV_DT].astype(BF16),
        "w_dt": w_dt.astype(BF16),
        "w_out": w_out.astype(BF16),
        "w_s": w_s.astype(BF16),
        "b_s": jnp.broadcast_to(b_s[:, :, None], b_s.shape + (LANES,)),
        "g_v": g_v.reshape(1, -1),
        "w_conv": w_conv,
        "b_conv": b_conv.reshape(1, -1),
        "dt_bias": narrow(dt_bias),
        "a_log": narrow(a_log),
        "d_skip": jnp.repeat(d_skip, SSM_HEAD_DIM, axis=1),
        "g_out": g_out.reshape(1, -1),
        "expand": expand,
    }


MLA_Q_RANK = 384
MLA_KV_RANK = 256
CONV_WIDTH = 512
HEAD_W = LANES
ROPE_L0 = MLA_NOPE
OD_CQ = 0
OD_CKV = OD_CQ + MLA_Q_RANK
OD_LAT = OD_CKV + MLA_KV_RANK
CONV_PAD = 16
CONV_RB = 64
ATT_QB = 256


def _odd_kernel(x_ref, mod_ref, g_ref, win_ref, wag_ref, wkr_ref, wuq_ref, wuk_ref, wuv_ref, wout_ref, gcq_ref, gckv_ref,
                wdw_ref, bdw_ref, gln_ref, bln_ref, *rest, seq_len, n_seq, past, n_carry, out_slot):
    rest = list(rest)
    has_cache = past > 0
    if has_cache:
        cckv_ref, ckr_ref, cos_ref, sin_ref = rest[:4]
        o_ref = rest[4]
        rest = rest[5:]
    else:
        rest = rest[n_carry:]
        o_ref, ckv_out_ref, kr_out_ref = rest[:3]
        rest = rest[3:]
    h_ref, cq_ref, ckv_ref, kr_ref, mix_ref, q_ref, k_ref, v_ref, oh_ref, pad_ref, cout_ref = rest

    nh = MLA_HEADS
    hw = nh * MLA_V
    att_qb = min(ATT_QB, seq_len)
    first_head_lanes = lax.broadcasted_iota(jnp.int32, (att_qb, HEAD_W), 1) < MLA_V
    n_keys = past + seq_len
    scale = (MLA_NOPE + MLA_ROPE) ** -0.5
    m = mod_ref[0, 0]
    h_ref[...] = (_rms(x_ref[...], g_ref[0]) * (1.0 + m[4:5]) + m[3:4]).astype(BF16)

    cq_ref[...] = _rms(_dot(h_ref[...], win_ref[:, OD_CQ:OD_CKV]), gcq_ref[...]).astype(BF16)
    ckv = _rms(_dot(h_ref[...], win_ref[:, OD_CKV:OD_LAT]), gckv_ref[...])
    ckv_ref[...] = ckv.astype(BF16)
    kr = _dot(h_ref[...], wkr_ref[:, 0:HEAD_W])
    if has_cache:
        cos = jnp.concatenate([cos_ref[...]] * n_seq, axis=0)
        sin = jnp.concatenate([sin_ref[...]] * n_seq, axis=0)
        kr = kr * cos + _dot(h_ref[...], wkr_ref[:, HEAD_W:2 * HEAD_W]) * sin
    else:
        for s in range(n_seq):
            ckv_out_ref[s, out_slot] = ckv[s * seq_len:(s + 1) * seq_len]
            kr_out_ref[s, out_slot] = kr[s * seq_len:(s + 1) * seq_len, ROPE_L0:ROPE_L0 + MLA_ROPE]
        for slot in range(ckv_out_ref.shape[1]):
            if slot != out_slot:
                ckv_out_ref[:, slot] = jnp.zeros((n_seq,) + ckv_out_ref.shape[2:], F32)
                kr_out_ref[:, slot] = jnp.zeros((n_seq,) + kr_out_ref.shape[2:], F32)
    kr_ref[...] = kr

    n_pair = nh // 2

    def twice(a):
        return jnp.concatenate([a, a], axis=1)

    def q_body(pr, carry):
        q = _dot(cq_ref[...], wuq_ref[pr])
        if has_cache:
            q = q * twice(cos) + _dot(cq_ref[...], wuq_ref[n_pair + pr]) * twice(sin)
        q_ref[pr] = q.astype(BF16)
        return carry

    lax.fori_loop(0, n_pair, q_body, 0, unroll=4)

    pw = 2 * LANES
    n_lt = CONV_WIDTH // LANES
    nv = CONV_RB // 8
    zpad = jnp.zeros((n_lt, CONV_PAD, LANES), F32)
    for s in range(n_seq):
        r0 = s * seq_len
        pad_ref[:, 0:CONV_PAD] = zpad
        pad_ref[:, CONV_PAD + seq_len:2 * CONV_PAD + seq_len] = zpad
        for c0 in range(0, CONV_WIDTH, pw):
            a = _dot(h_ref[r0:r0 + seq_len], wag_ref[:, c0:c0 + pw])
            gt = _dot(h_ref[r0:r0 + seq_len], wag_ref[:, CONV_WIDTH + c0:CONV_WIDTH + c0 + pw])
            glu = a * _sigmoid(gt)
            for l0 in range(0, pw, LANES):
                pad_ref[(c0 + l0) // LANES, CONV_PAD:CONV_PAD + seq_len] = glu[:, l0:l0 + LANES]

        def conv_body(b, carry, r0=r0):
            base = pl.multiple_of(b * CONV_RB, CONV_RB)
            for lt in range(n_lt):
                l0 = lt * LANES
                accs = [jnp.broadcast_to(bdw_ref[:, l0:l0 + LANES], (8, LANES))] * nv
                for k in range(CONV_K):
                    wk = jnp.broadcast_to(wdw_ref[k:k + 1, l0:l0 + LANES], (8, LANES))
                    off = k + CONV_PAD - CONV_K // 2
                    accs = [accs[v] + wk * pad_ref[lt, pl.ds(base + v + off, 8, stride=nv), :] for v in range(nv)]
                for v in range(nv):
                    cout_ref[lt, pl.ds(base + v, 8, stride=nv), :] = accs[v]
            acc = jnp.concatenate([cout_ref[lt, pl.ds(base, CONV_RB), :] for lt in range(n_lt)], axis=1)
            dc = acc - jnp.mean(acc, axis=-1, keepdims=True)
            var = jnp.mean(dc * dc, axis=-1, keepdims=True)
            dn = dc * lax.rsqrt(var + EPS) * gln_ref[...] + bln_ref[...]
            mix_ref[pl.ds(r0 + base, CONV_RB), hw:] = _silu(dn).astype(BF16)
            return carry

        lax.fori_loop(0, seq_len // CONV_RB, conv_body, 0, unroll=seq_len // CONV_RB)

    n_qb = seq_len // att_qb
    for s in range(n_seq):
        r0 = s * seq_len

        def kv_body(pr, carry, s=s, r0=r0):
            lat = ckv_ref[r0:r0 + seq_len]
            k_ref[pr, past:n_keys] = (_dot(lat, wuk_ref[pr]) + twice(kr_ref[r0:r0 + seq_len])).astype(BF16)
            v_ref[pr, past:n_keys] = _dot(lat, wuv_ref[pr]).astype(BF16)
            if has_cache:
                lat_c = cckv_ref[s, 0].astype(BF16)
                k_ref[pr, 0:past] = (_dot(lat_c, wuk_ref[pr]) + twice(ckr_ref[s, 0])).astype(BF16)
                v_ref[pr, 0:past] = _dot(lat_c, wuv_ref[pr]).astype(BF16)
            return carry

        lax.fori_loop(0, n_pair, kv_body, 0, unroll=4)

        def att_body(it, carry, r0=r0):
            pr = it // n_qb
            q0 = pl.multiple_of(r0 + (it % n_qb) * att_qb, att_qb)
            outs = []
            for l0 in range(0, 2 * HEAD_W, HEAD_W):
                sc = _dot_nt(q_ref[pr, pl.ds(q0, att_qb), l0:l0 + HEAD_W], k_ref[pr, :, l0:l0 + HEAD_W]) * scale
                p = jnp.exp(sc - jnp.max(sc, axis=-1, keepdims=True))
                den = jnp.sum(p, axis=-1, keepdims=True)
                outs.append(_dot(p.astype(BF16), v_ref[pr]) / den)
            oh_ref[pr, pl.ds(q0, att_qb), :] = jnp.where(first_head_lanes, outs[0], outs[1]).astype(BF16)
            return carry

        lax.fori_loop(0, n_pair * n_qb, att_body, 0, unroll=4)

    for pr in range(n_pair):
        mix_ref[:, pr * HEAD_W:(pr + 1) * HEAD_W] = oh_ref[pr]

    for c0 in range(0, o_ref.shape[1], pw):
        o_ref[:, c0:c0 + pw] = (x_ref[:, c0:c0 + pw]
                                + m[5:6, c0:c0 + pw] * _dot(mix_ref[...], wout_ref[:, c0:c0 + pw]))


def _odd_call(x, mod_l, g, p, cache, *, layer, layer_idx, seq_len, n_seq, row0, n_tiles, mod_set, batch,
              n_layers=1, carry=None):
    t, d = x.shape
    rows = seq_len * n_seq
    tile0 = row0 // rows
    has_cache = cache is not None
    past = cache[0].shape[2] if has_cache else 0
    names = ["w_in", "w_ag", "w_kr", "w_uq", "w_uk", "w_uv", "w_out", "g_cq", "g_ckv", "w_dw", "b_dw", "g_ln", "b_ln"]
    in_specs = [
        pl.BlockSpec((rows, d), lambda i: (tile0 + i, 0)),
        _mod_spec(mod_l, layer, mod_set),
        _gain_spec(*g),
    ] + [_resident(p[k].shape) for k in names]
    args = [x, mod_l, g[0]] + [p[k] for k in names]
    out_shape = [jax.ShapeDtypeStruct((t, d), F32)]
    out_specs = [pl.BlockSpec((rows, d), lambda i: (tile0 + i, 0))]
    aliases = {0: 0}
    if has_cache:
        cckv, ckr, cos, sin = cache
        in_specs += [
            pl.BlockSpec((n_seq, 1, past, MLA_KV_RANK), lambda i: (i, layer_idx, 0, 0)),
            pl.BlockSpec((n_seq, 1, past, HEAD_W), lambda i: (i, layer_idx, 0, 0)),
            _resident(cos.shape),
            _resident(sin.shape),
        ]
        args += [cckv, ckr, cos, sin]
    else:
        out_shape += [jax.ShapeDtypeStruct((batch, n_layers, seq_len, MLA_KV_RANK), F32),
                      jax.ShapeDtypeStruct((batch, n_layers, seq_len, MLA_ROPE), F32)]
        if carry is None:
            out_specs += [pl.BlockSpec((n_seq, n_layers, seq_len, MLA_KV_RANK), lambda i: (i, 0, 0, 0)),
                          pl.BlockSpec((n_seq, n_layers, seq_len, MLA_ROPE), lambda i: (i, 0, 0, 0))]
        else:
            out_specs += [pl.BlockSpec((n_seq, 1, seq_len, MLA_KV_RANK), lambda i: (i, layer_idx, 0, 0)),
                          pl.BlockSpec((n_seq, 1, seq_len, MLA_ROPE), lambda i: (i, layer_idx, 0, 0))]
            for k, arr in enumerate(carry):
                in_specs.append(pl.BlockSpec(memory_space=pl.ANY))
                args.append(arr)
                aliases[len(args) - 1] = 1 + k
    n_keys = past + seq_len
    scratch = [
        pltpu.VMEM((rows, d), BF16),
        pltpu.VMEM((rows, MLA_Q_RANK), BF16),
        pltpu.VMEM((rows, MLA_KV_RANK), BF16),
        pltpu.VMEM((rows, HEAD_W), F32),
        pltpu.VMEM((rows, MLA_HEADS * MLA_V + CONV_WIDTH), BF16),
        pltpu.VMEM((MLA_HEADS // 2, rows, 2 * HEAD_W), BF16),
        pltpu.VMEM((MLA_HEADS // 2, n_keys, 2 * HEAD_W), BF16),
        pltpu.VMEM((MLA_HEADS // 2, n_keys, 2 * MLA_V), BF16),
        pltpu.VMEM((MLA_HEADS // 2, rows, 2 * MLA_V), BF16),
        pltpu.VMEM((CONV_WIDTH // LANES, seq_len + 2 * CONV_PAD, LANES), F32),
        pltpu.VMEM((CONV_WIDTH // LANES, seq_len, LANES), F32),
    ]
    return pl.pallas_call(
        functools.partial(_odd_kernel, seq_len=seq_len, n_seq=n_seq, past=past, n_carry=len(aliases) - 1,
                          out_slot=layer_idx if (not has_cache and carry is None) else 0),
        out_shape=out_shape,
        grid=(n_tiles,),
        in_specs=in_specs,
        out_specs=out_specs,
        scratch_shapes=scratch,
        input_output_aliases=aliases,
        compiler_params=pltpu.CompilerParams(
            dimension_semantics=("arbitrary",), vmem_limit_bytes=VMEM_LIMIT),
        name="odd_mixer_latent" if has_cache else "odd_mixer_context",
    )(*args)


def _head_tiles(w, per_head, start, width, lane0):
    k = w.shape[0]
    blk = w.reshape(k, MLA_HEADS, per_head)[:, :, start:start + width]
    out = jnp.zeros((k, MLA_HEADS, HEAD_W), w.dtype).at[:, :, lane0:lane0 + width].set(blk)
    return out.reshape(k, MLA_HEADS // 2, 2 * HEAD_W).transpose(1, 0, 2)


def _rope_swap(w):
    q = MLA_ROPE // 4
    idx = jnp.concatenate([jnp.arange(q, 2 * q), jnp.arange(0, q), jnp.arange(3 * q, 4 * q), jnp.arange(2 * q, 3 * q)])
    return w[..., idx]


def _pack_odd(w_in, w_out, g_cq, w_uq, g_ckv, w_ukv, w_dw, b_dw, g_ln, b_ln):
    d = w_in.shape[0]
    o1 = MLA_Q_RANK
    o2 = o1 + MLA_KV_RANK
    o3 = o2 + MLA_ROPE
    w_kr = w_in[:, o2:o3]

    def kr_tile(w):
        return jnp.zeros((d, HEAD_W), F32).at[:, ROPE_L0:ROPE_L0 + MLA_ROPE].set(w)

    w_kr_p = jnp.concatenate([kr_tile(w_kr), kr_tile(_rope_swap(w_kr))], axis=1)
    qh = MLA_NOPE + MLA_ROPE
    rank = w_uq.shape[0]
    w_q3 = w_uq.reshape(rank, MLA_HEADS, qh)
    w_q_sw = jnp.concatenate([jnp.zeros((rank, MLA_HEADS, MLA_NOPE), F32), _rope_swap(w_q3[:, :, MLA_NOPE:])],
                             axis=2).reshape(rank, MLA_HEADS * qh)
    w_uq_p = jnp.concatenate([_head_tiles(w_uq, qh, 0, qh, 0),
                              _head_tiles(w_q_sw, qh, MLA_NOPE, MLA_ROPE, ROPE_L0)], axis=0)
    kvh = MLA_NOPE + MLA_V
    w_uk_p = _head_tiles(w_ukv, kvh, 0, MLA_NOPE, 0)
    rank_kv = w_ukv.shape[0]
    w_uv_p = (w_ukv.reshape(rank_kv, MLA_HEADS, kvh)[:, :, MLA_NOPE:]
              .reshape(rank_kv, MLA_HEADS // 2, 2 * MLA_V).transpose(1, 0, 2))
    return {
        "w_in": w_in[:, :o2].astype(BF16),
        "w_ag": w_in[:, o3:].astype(BF16),
        "w_kr": w_kr_p.astype(BF16),
        "w_uq": w_uq_p.astype(BF16),
        "w_uk": w_uk_p.astype(BF16),
        "w_uv": w_uv_p.astype(BF16),
        "w_out": w_out.astype(BF16),
        "g_cq": g_cq.reshape(1, -1),
        "g_ckv": g_ckv.reshape(1, -1),
        "w_dw": w_dw,
        "b_dw": b_dw.reshape(1, -1),
        "g_ln": g_ln.reshape(1, -1),
        "b_ln": b_ln.reshape(1, -1),
    }


def _rope_tables(length):
    n_freq = MLA_ROPE // 4
    pos = jnp.arange(length)
    row = (pos // GRID_W).astype(F32)
    col = (pos % GRID_W).astype(F32)
    freqs = ROPE_THETA ** (-jnp.arange(n_freq, dtype=F32) / n_freq)
    ar = row[:, None] * freqs
    ac = col[:, None] * freqs
    cos = jnp.concatenate([jnp.cos(ar), jnp.cos(ar), jnp.cos(ac), jnp.cos(ac)], axis=1)
    sin = jnp.concatenate([-jnp.sin(ar), jnp.sin(ar), -jnp.sin(ac), jnp.sin(ac)], axis=1)
    tail = HEAD_W - ROPE_L0 - MLA_ROPE
    cos_t = jnp.concatenate([jnp.ones((length, ROPE_L0), F32), cos, jnp.zeros((length, tail), F32)], axis=1)
    sin_t = jnp.concatenate([jnp.zeros((length, ROPE_L0), F32), sin, jnp.zeros((length, tail), F32)], axis=1)
    return cos_t, sin_t


def kernel(x_prompt, x_sample, state_ssd, cache_mla_ckv, cache_mla_krope, c, c_ctx, w_mod, b_mod, g_norm, w_ff_gu, w_ff_down, w_in_even, w_out_even, w_spatial, b_spatial, g_gmlp_v, w_conv_ssm, b_conv_ssm, dt_bias, a_log, d_skip, g_ssm_out, w_in_odd, w_out_odd, g_cq, w_uq, g_ckv, w_ukv, w_dwconv, b_dwconv, g_conv_ln, b_conv_ln, g_final):
    bp, seq, d = x_prompt.shape
    db, dec_seq, _ = x_sample.shape
    n_ctx = bp * seq
    cond_t = jnp.concatenate([c_ctx[None], c], axis=0).T
    mod = _mod_call(cond_t, w_mod, b_mod).reshape(w_mod.shape[0], 1 + db, N_MOD, d)
    depth = w_mod.shape[0]
    ffn = functools.partial(_ffn_call, tm=512, n_ctx_rows=n_ctx, dec_seq=dec_seq)
    ctx_seqs = 4
    ctx_tile = dict(seq_len=seq, n_seq=ctx_seqs, row0=0, n_tiles=bp // ctx_seqs, mod_set=lambda i: 0, batch=bp)
    lat_tile = dict(seq_len=dec_seq, n_seq=1, row0=n_ctx, n_tiles=db, mod_set=lambda i: 1 + i, batch=bp)
    h0 = state_ssd.reshape(db, state_ssd.shape[1], 2, SSM_INNER, SSM_STATE)
    ckr_tile = jnp.pad(cache_mla_krope, ((0, 0), (0, 0), (0, 0), (ROPE_L0, HEAD_W - ROPE_L0 - MLA_ROPE)))
    cos_t, sin_t = _rope_tables(dec_seq)
    n_even, n_odd = (depth + 1) // 2, depth // 2
    new_state = new_cache = None
    xs = [x_prompt.reshape(n_ctx, d), x_sample.reshape(db * dec_seq, d)]
    n_gain = g_norm.shape[1]
    gains = g_norm.reshape(depth * n_gain, 1, d)
    for l in range(depth):
        i = l // 2
        last = l == depth - 1
        x = ffn(xs, mod, (gains, l * n_gain), w_ff_gu, w_ff_down, None, layer=l, which=0, mrow=0)
        g_mix = (gains, l * n_gain + 1)
        if l % 2 == 0:
            pe = _pack_even(w_in_even[i], w_out_even[i], w_spatial[i], b_spatial[i], g_gmlp_v[i], w_conv_ssm[i],
                            b_conv_ssm[i], dt_bias[i], a_log[i], d_skip[i], g_ssm_out[i])
            x, new_state = _even_call(x, mod, g_mix, pe, None, layer=l, layer_idx=i, n_layers=n_even,
                                      carry=new_state, **ctx_tile)
            (x,) = _even_call(x, mod, g_mix, pe, h0, layer=l, layer_idx=i, **lat_tile)
        else:
            po = _pack_odd(w_in_odd[i], w_out_odd[i], g_cq[i], w_uq[i], g_ckv[i], w_ukv[i], w_dwconv[i],
                           b_dwconv[i], g_conv_ln[i], b_conv_ln[i])
            x, *new_cache = _odd_call(x, mod, g_mix, po, None, layer=l, layer_idx=i, n_layers=n_odd,
                                      carry=new_cache, **ctx_tile)
            (x,) = _odd_call(x, mod, g_mix, po, (cache_mla_ckv, ckr_tile, cos_t, sin_t),
                             layer=l, layer_idx=i, **lat_tile)
        res = ffn([x], mod, (gains, l * n_gain + 2), w_ff_gu, w_ff_down, g_final if last else None,
                  layer=l, which=1, mrow=6)
        xs = [res]
    y_ctx, y_lat = res
    new_state = new_state.reshape(bp, n_even, 2, SSM_HEADS, SSM_HEAD_DIM, SSM_STATE)
    return (y_ctx.reshape(bp, seq, d), y_lat.reshape(db, dec_seq, d), new_state, new_cache[0], new_cache[1])
```
